```python
import jax, jax.numpy as jnp
from jax import lax
import numpy as np

D_MODEL = 2048
BATCH = 4
SEQ = 4096
DEPTH = 4

MEM_LEN = 256
G_CHUNK = 128
G_GROUPS = 8
G_GROUP_DIM = D_MODEL // 16
G_WIDTH = G_GROUPS * G_GROUP_DIM
M_HEADS = 4
M_QK_DIM = D_MODEL // 8
M_V_DIM = D_MODEL // 4
M_QK_WIDTH = M_HEADS * M_QK_DIM
M_V_WIDTH = M_HEADS * M_V_DIM
M_CHUNK = 64
M_CONV = 4
X_HEADS = 4
X_HEAD_DIM = D_MODEL // 8
X_WIDTH = X_HEADS * X_HEAD_DIM
N_BRANCH = 3
D_FF = ((-(-8 * D_MODEL // 3) + 255) // 256) * 256
ALPHA = (2 * DEPTH) ** 0.25
BETA = (8 * DEPTH) ** -0.25
IN_SPLITS = (G_WIDTH, G_WIDTH, 2 * M_QK_WIDTH, M_V_WIDTH, M_V_WIDTH,
             M_HEADS, M_HEADS, X_WIDTH, N_BRANCH * D_MODEL)
IN_WIDTH = sum(IN_SPLITS)
IN_OFFSETS = tuple(int(o) for o in np.cumsum(IN_SPLITS)[:-1])
F_GATE_OFFSET = sum(IN_SPLITS[:6])
LN_EPS = 1e-5

kernel_name = 'gmlp_mlstm_memxattn_deepnorm_hybrid'


def layer_norm(x, g, b):
    xf = x.astype(jnp.float32)
    mu = jnp.mean(xf, -1, keepdims=True)
    var = jnp.mean(jnp.square(xf - mu), -1, keepdims=True)
    return ((xf - mu) * lax.rsqrt(var + LN_EPS) * g + b).astype(x.dtype)


def head_norm(h):
    hf = h.astype(jnp.float32)
    mu = jnp.mean(hf, -1, keepdims=True)
    var = jnp.mean(jnp.square(hf - mu), -1, keepdims=True)
    return (hf - mu) * lax.rsqrt(var + LN_EPS)


def causal_depthwise_conv(x, w, b):
    k = w.shape[0]
    y = lax.conv_general_dilated(x, w[:, None, :], window_strides=(1,),
                                 padding=[(k - 1, 0)],
                                 dimension_numbers=('NWC', 'WIO', 'NWC'),
                                 feature_group_count=x.shape[-1])
    return y + b


def chunked_spatial_gating(u, v, ln_g, ln_b, w_s, b_s):
    B, S, _ = v.shape
    v = layer_norm(v, ln_g, ln_b)
    vc = v.reshape(B, S // G_CHUNK, G_CHUNK, G_GROUPS, G_GROUP_DIM)
    causal = jnp.tril(jnp.ones((G_CHUNK, G_CHUNK), dtype=bool))
    w = jnp.where(causal, w_s, 0)
    mixed = jnp.einsum('gts,bcsgd->bctgd', w, vc) + b_s.T[None, None, :, :, None]
    return u * mixed.reshape(B, S, G_WIDTH)


def mlstm_chunkwise(q, k, v, i_pre, f_pre):
    B, S, H, dk = q.shape
    dv = v.shape[-1]
    L = M_CHUNK
    NC = S // L
    f32 = jnp.float32

    def to_chunks(t):
        t = t.astype(f32).reshape((B, NC, L, H) + t.shape[3:])
        return jnp.moveaxis(t, (1, 3), (0, 2))

    qc = to_chunks(q)
    kc = to_chunks(k) * (dk ** -0.5)
    vc = to_chunks(v)
    ic = to_chunks(i_pre)
    lfc = to_chunks(jax.nn.log_sigmoid(f_pre.astype(f32)))
    causal = jnp.tril(jnp.ones((L, L), dtype=bool))

    def step(carry, xs):
        C, n, m = carry
        qb, kb, vb, ib, lfb = xs
        b = jnp.cumsum(lfb, axis=-1)
        D = jnp.where(causal, b[..., :, None] - b[..., None, :] + ib[..., None, :], -jnp.inf)
        m_inter = b + m[..., None]
        m_row = jnp.maximum(jnp.max(D, -1), m_inter)
        P = jnp.exp(D - m_row[..., None]) * jnp.einsum('bhtk,bhsk->bhts', qb, kb)
        w_inter = jnp.exp(m_inter - m_row)
        num = (jnp.einsum('bhts,bhsv->bhtv', P, vb)
               + w_inter[..., None] * jnp.einsum('bhtk,bhkv->bhtv', qb, C))
        den = jnp.sum(P, -1) + w_inter * jnp.einsum('bhtk,bhk->bht', qb, n)
        h = num / jnp.maximum(jnp.abs(den), jnp.exp(-m_row))[..., None]
        b_last = b[..., -1]
        g = b_last[..., None] - b + ib
        m_new = jnp.maximum(b_last + m, jnp.max(g, -1))
        w_state = jnp.exp(g - m_new[..., None])
        decay = jnp.exp(b_last + m - m_new)
        C = decay[..., None, None] * C + jnp.einsum('bhs,bhsk,bhsv->bhkv', w_state, kb, vb)
        n = decay[..., None] * n + jnp.einsum('bhs,bhsk->bhk', w_state, kb)
        return (C, n, m_new), h

    init = (jnp.zeros((B, H, dk, dv), f32), jnp.zeros((B, H, dk), f32), jnp.zeros((B, H), f32))
    _, h = lax.scan(step, init, (qc, kc, vc, ic, lfc))
    return jnp.moveaxis(h, (0, 2), (1, 3)).reshape(B, S, H, dv)


def memory_cross_attention(q, k, v):
    s = jnp.einsum('bshd,bmhd->bhsm', q, k).astype(jnp.float32) * (q.shape[-1] ** -0.5)
    p = jax.nn.softmax(s, axis=-1).astype(v.dtype)
    return jnp.einsum('bhsm,bmhd->bshd', p, v)


def token_mixer(x, mem_n, w_in, b_in, g_ln_g, g_ln_b, g_ws, g_bs, m_conv_w, m_conv_b,
                m_norm_g, x_w_kv, w_pa, w_pb, w_pc, w_out):
    B, S, _ = x.shape
    z = x @ w_in + b_in
    zu, zv, zqk, zvm, zo, zi, zf, zqx, zg = jnp.split(z, IN_OFFSETS, axis=-1)
    y_a = chunked_spatial_gating(jax.nn.gelu(zu), jax.nn.gelu(zv), g_ln_g, g_ln_b, g_ws, g_bs)
    qk = jax.nn.silu(causal_depthwise_conv(zqk, m_conv_w, m_conv_b))
    q_m, k_m = jnp.split(qk, 2, axis=-1)
    h = mlstm_chunkwise(q_m.reshape(B, S, M_HEADS, M_QK_DIM), k_m.reshape(B, S, M_HEADS, M_QK_DIM),
                        zvm.reshape(B, S, M_HEADS, M_V_DIM), zi, zf)
    h = head_norm(h).reshape(B, S, M_V_WIDTH) * m_norm_g
    y_b = (jax.nn.sigmoid(zo) * h).astype(x.dtype)
    k_x, v_x = jnp.split(mem_n @ x_w_kv, 2, axis=-1)
    y_c = memory_cross_attention(zqx.reshape(B, S, X_HEADS, X_HEAD_DIM),
                                 k_x.reshape(B, -1, X_HEADS, X_HEAD_DIM),
                                 v_x.reshape(B, -1, X_HEADS, X_HEAD_DIM)).reshape(B, S, X_WIDTH)
    gates = jax.nn.sigmoid(zg).reshape(B, S, N_BRANCH, D_MODEL)
    merged = (gates[:, :, 0] * (y_a @ w_pa) + gates[:, :, 1] * (y_b @ w_pb)
              + gates[:, :, 2] * (y_c @ w_pc))
    return merged @ w_out


def swiglu_ffn(x, w_gu, w_down):
    gate, up = jnp.split(x @ w_gu, 2, axis=-1)
    return (jax.nn.silu(gate) * up) @ w_down


def setup_inputs(seed: int = 0) -> dict:
    key = jax.random.key(seed)
    keys = iter(jax.random.split(key, 32))
    L = DEPTH

    def nrm(shape, scale):
        return scale * jax.random.normal(next(keys), shape, jnp.float32)

    x = nrm((BATCH, SEQ, D_MODEL), 1.0)
    mem = nrm((BATCH, MEM_LEN, D_MODEL), 1.0)
    mem_ln_g = 1.0 + nrm((D_MODEL,), 0.02)
    mem_ln_b = nrm((D_MODEL,), 0.02)
    w_in = nrm((L, D_MODEL, IN_WIDTH), D_MODEL ** -0.5)
    b_in = nrm((L, IN_WIDTH), 0.02)
    f_bias = jnp.linspace(3.0, 6.0, M_HEADS, dtype=jnp.float32)
    b_in = b_in.at[:, F_GATE_OFFSET:F_GATE_OFFSET + M_HEADS].add(f_bias)
    g_ln_g = 1.0 + nrm((L, G_WIDTH), 0.02)
    g_ln_b = nrm((L, G_WIDTH), 0.02)
    g_ws = nrm((L, G_GROUPS, G_CHUNK, G_CHUNK), G_CHUNK ** -0.5)
    g_bs = 1.0 + nrm((L, G_GROUPS, G_CHUNK), 0.02)
    m_conv_w = nrm((L, M_CONV, 2 * M_QK_WIDTH), M_CONV ** -0.5)
    m_conv_b = nrm((L, 2 * M_QK_WIDTH), 0.02)
    m_norm_g = 1.0 + nrm((L, M_V_WIDTH), 0.02)
    x_w_kv = nrm((L, D_MODEL, 2 * X_WIDTH), D_MODEL ** -0.5)
    w_pa = nrm((L, G_WIDTH, D_MODEL), G_WIDTH ** -0.5)
    w_pb = nrm((L, M_V_WIDTH, D_MODEL), M_V_WIDTH ** -0.5)
    w_pc = nrm((L, X_WIDTH, D_MODEL), X_WIDTH ** -0.5)
    w_out = nrm((L, D_MODEL, D_MODEL), BETA * D_MODEL ** -0.5)
    ln1_g = 1.0 + nrm((L, D_MODEL), 0.02)
    ln1_b = nrm((L, D_MODEL), 0.02)
    w_gu = nrm((L, D_MODEL, 2 * D_FF), D_MODEL ** -0.5)
    w_down = nrm((L, D_FF, D_MODEL), BETA * D_FF ** -0.5)
    ln2_g = 1.0 + nrm((L, D_MODEL), 0.02)
    ln2_b = nrm((L, D_MODEL), 0.02)
    return {'x': x, 'mem': mem, 'mem_ln_g': mem_ln_g, 'mem_ln_b': mem_ln_b,
            'w_in': w_in, 'b_in': b_in, 'g_ln_g': g_ln_g, 'g_ln_b': g_ln_b,
            'g_ws': g_ws, 'g_bs': g_bs, 'm_conv_w': m_conv_w, 'm_conv_b': m_conv_b,
            'm_norm_g': m_norm_g, 'x_w_kv': x_w_kv, 'w_pa': w_pa, 'w_pb': w_pb,
            'w_pc': w_pc, 'w_out': w_out, 'ln1_g': ln1_g, 'ln1_b': ln1_b,
            'w_gu': w_gu, 'w_down': w_down, 'ln2_g': ln2_g, 'ln2_b': ln2_b}


def reference(x, mem, mem_ln_g, mem_ln_b, w_in, b_in, g_ln_g, g_ln_b, g_ws, g_bs,
              m_conv_w, m_conv_b, m_norm_g, x_w_kv, w_pa, w_pb, w_pc, w_out,
              ln1_g, ln1_b, w_gu, w_down, ln2_g, ln2_b):
    mem_n = layer_norm(mem, mem_ln_g, mem_ln_b)
    for l in range(DEPTH):
        mix = token_mixer(x, mem_n, w_in[l], b_in[l], g_ln_g[l], g_ln_b[l], g_ws[l], g_bs[l],
                          m_conv_w[l], m_conv_b[l], m_norm_g[l], x_w_kv[l],
                          w_pa[l], w_pb[l], w_pc[l], w_out[l])
        x = layer_norm(ALPHA * x + mix, ln1_g[l], ln1_b[l])
        x = layer_norm(ALPHA * x + swiglu_ffn(x, w_gu[l], w_down[l]), ln2_g[l], ln2_b[l])
    return x
```

```python
import functools
import math

import jax
import jax.numpy as jnp
from jax import lax
from jax.experimental import pallas as pl
from jax.experimental.pallas import tpu as pltpu

F32 = jnp.float32
BF16 = jnp.bfloat16

LN_EPS = 1e-5
LANES = 128
SUBLANES = 8
VMEM_LIMIT_BYTES = 56 * 1024 * 1024

G_CHUNK = 128
G_GROUPS = 8
M_HEADS = 4
M_CONV = 4
M_SCAN_CHUNK = 256
X_HEADS = 4
N_BRANCH = 3


def _params(*sem):
    return pltpu.CompilerParams(dimension_semantics=sem, vmem_limit_bytes=VMEM_LIMIT_BYTES)


def _tile(n, pref):
    t = min(n, pref)
    assert n % t == 0, (n, pref)
    return t


def _sigmoid(x):
    return 1.0 / (1.0 + jnp.exp(-x))


def _silu(x):
    return x * _sigmoid(x)


def _gelu_tanh(x):
    c = math.sqrt(2.0 / math.pi)
    return x * (0.5 * (1.0 + jnp.tanh(c * (x + 0.044715 * (x * x * x)))))


def _log_sigmoid(x):
    return jnp.minimum(x, 0.0) - jnp.log1p(jnp.exp(-jnp.abs(x)))


def _normalize(x):
    mu = jnp.mean(x, axis=-1, keepdims=True)
    xc = x - mu
    var = jnp.mean(xc * xc, axis=-1, keepdims=True)
    return xc * lax.rsqrt(var + LN_EPS)


def _dot(a, b):
    return jnp.dot(a, b, preferred_element_type=F32)


def _dot_nt(a, b):
    return lax.dot_general(a, b, (((1,), (1,)), ((), ())), preferred_element_type=F32)


def _dot_tn(a, b):
    return lax.dot_general(a, b, (((0,), (0,)), ((), ())), preferred_element_type=F32)


def _linear(x, w, bias, layer, col0, ncols, *, tm, tn, epilogue, outs, extras=(), scratch=(),
            row_sem="parallel", name):
    n, k = x.shape
    tm = _tile(n, tm)
    assert ncols % tn == 0 and col0 % tn == 0
    j0 = col0 // tn
    grid = (ncols // tn, n // tm)
    has_bias = bias is not None
    in_specs = [pl.BlockSpec((tm, k), lambda j, i: (i, 0)),
                pl.BlockSpec((None, k, tn), lambda j, i: (layer, 0, j0 + j))]
    args = [x, w]
    if has_bias:
        in_specs.append(pl.BlockSpec((None, 1, tn), lambda j, i: (layer, 0, j0 + j)))
        args.append(bias)
    for arr, spec in extras:
        in_specs.append(spec)
        args.append(arr)
    out_shape = [jax.ShapeDtypeStruct((n, tot), dt) for tot, _, dt in outs]
    out_specs = [pl.BlockSpec((tm, bc), lambda j, i: (i, j)) for _, bc, _ in outs]
    n_extra, n_out = len(extras), len(outs)

    def body(*refs):
        x_ref, w_ref = refs[0], refs[1]
        pos = 2
        acc = _dot(x_ref[...], w_ref[...])
        if has_bias:
            acc = acc + refs[pos][...]
            pos += 1
        extra_refs = refs[pos:pos + n_extra]
        out_refs = refs[pos + n_extra:pos + n_extra + n_out]
        scratch_refs = refs[pos + n_extra + n_out:]
        epilogue(acc, extra_refs, out_refs, scratch_refs)

    res = pl.pallas_call(
        body, out_shape=out_shape, grid=grid, in_specs=in_specs, out_specs=out_specs,
        scratch_shapes=list(scratch), compiler_params=_params("parallel", row_sem), name=name,
    )(*args)
    return res


def _epi_act(act, acc, extra_refs, out_refs, scratch_refs):
    out_refs[0][...] = act(acc).astype(out_refs[0].dtype)


def _epi_gelu_ln(acc, extra_refs, out_refs, scratch_refs):
    g_ref, b_ref = extra_refs
    y = _normalize(_gelu_tanh(acc)) * g_ref[...] + b_ref[...]
    out_refs[0][...] = y.astype(out_refs[0].dtype)


def _epi_conv_silu(tiles_per_seq, acc, extra_refs, out_refs, scratch_refs):
    cw_ref, cb_ref = extra_refs
    carry_ref, = scratch_refs
    i = pl.program_id(1)

    @pl.when(i % tiles_per_seq == 0)
    def _():
        carry_ref[...] = jnp.zeros_like(carry_ref)

    tm = acc.shape[0]
    carry = carry_ref[...]
    cw = cw_ref[...]
    y = acc * cw[M_CONV - 1:M_CONV, :] + cb_ref[...]
    top_rows = lax.broadcasted_iota(jnp.int32, (SUBLANES, acc.shape[1]), 0)
    for s in range(1, M_CONV):
        shifted = pltpu.roll(acc, s, axis=0)
        top = jnp.where(top_rows < s, pltpu.roll(carry, s, axis=0), shifted[:SUBLANES])
        shifted = jnp.concatenate([top, shifted[SUBLANES:]], axis=0)
        y = y + shifted * cw[M_CONV - 1 - s:M_CONV - s, :]
    carry_ref[...] = acc[tm - SUBLANES:, :]
    out_refs[0][...] = _silu(y).astype(out_refs[0].dtype)


def _layernorm_rows(x, g, b, *, tm, out_dtype, name):
    n, d = x.shape
    tm = _tile(n, tm)

    def body(x_ref, g_ref, b_ref, o_ref):
        o_ref[...] = (_normalize(x_ref[...]) * g_ref[...] + b_ref[...]).astype(o_ref.dtype)

    return pl.pallas_call(
        body, out_shape=jax.ShapeDtypeStruct((n, d), out_dtype), grid=(n // tm,),
        in_specs=[pl.BlockSpec((tm, d), lambda i: (i, 0)),
                  pl.BlockSpec((1, d), lambda i: (0, 0)),
                  pl.BlockSpec((1, d), lambda i: (0, 0))],
        out_specs=pl.BlockSpec((tm, d), lambda i: (i, 0)),
        compiler_params=_params("parallel"), name=name,
    )(x, g.reshape(1, d), b.reshape(1, d))


def _split3_bf16(x):
    h1 = x.astype(BF16)
    r1 = x - h1.astype(F32)
    h2 = r1.astype(BF16)
    r2 = r1 - h2.astype(F32)
    return h1, h2, r2.astype(BF16)


def _mlstm_gates(x, w_col, w_row, b_col, b_row, layer, *, tm, chunk, name):
    n, k = x.shape
    tm = _tile(n, tm)
    assert tm % chunk == 0
    heads = M_HEADS

    def body(x_ref, wc_ref, wr_ref, bc_ref, br_ref, gcol_ref, grow_ref):
        xv = x_ref[...]
        zc = _dot(xv, wc_ref[...]) + bc_ref[...]
        zr = _dot_nt(wr_ref[...], xv)[:2 * heads] + br_ref[...]
        lfc = _log_sigmoid(zc)
        lfr = _log_sigmoid(zr)
        r = lax.broadcasted_iota(jnp.int32, (chunk, chunk), 0)
        c = lax.broadcasted_iota(jnp.int32, (chunk, chunk), 1)
        tril = jnp.where(r >= c, 1.0, 0.0).astype(BF16)
        triu = jnp.where(r <= c, 1.0, 0.0).astype(BF16)
        lane = lax.broadcasted_iota(jnp.int32, (chunk, LANES), 1)
        is_f_lane = (lane >= heads) & (lane < 2 * heads)
        sub = lax.broadcasted_iota(jnp.int32, (2 * heads, chunk), 0)
        for ci in range(tm // chunk):
            sl = slice(ci * chunk, (ci + 1) * chunk)
            a1, a2, a3 = _split3_bf16(lfc[sl])
            cs = _dot(tril, a1) + _dot(tril, a2) + _dot(tril, a3)
            gcol_ref[sl, :] = jnp.where(is_f_lane, cs, zc[sl])
            r1, r2, r3 = _split3_bf16(lfr[:, sl])
            rs = _dot(r1, triu) + _dot(r2, triu) + _dot(r3, triu)
            grow_ref[:, sl] = jnp.where(sub >= heads, rs, zr[:, sl])

    return pl.pallas_call(
        body,
        out_shape=[jax.ShapeDtypeStruct((n, LANES), F32), jax.ShapeDtypeStruct((2 * heads, n), F32)],
        grid=(n // tm,),
        in_specs=[pl.BlockSpec((tm, k), lambda i: (i, 0)),
                  pl.BlockSpec((None, k, LANES), lambda i: (layer, 0, 0)),
                  pl.BlockSpec((None, 16, k), lambda i: (layer, 0, 0)),
                  pl.BlockSpec((None, 1, LANES), lambda i: (layer, 0, 0)),
                  pl.BlockSpec((None, 2 * heads, 1), lambda i: (layer, 0, 0))],
        out_specs=[pl.BlockSpec((tm, LANES), lambda i: (i, 0)),
                   pl.BlockSpec((2 * heads, tm), lambda i: (0, i))],
        compiler_params=_params("parallel"), name=name,
    )(x, w_col, w_row, b_col, b_row)


def _spatial_gating(u, v, w_s, b_s_t, layer, *, tm, name):
    n, width = u.shape
    tm = _tile(n, tm)
    gd = width // G_GROUPS
    assert tm % G_CHUNK == 0 and gd % LANES == 0

    def body(u_ref, v_ref, w_ref, b_ref, y_ref):
        r = lax.broadcasted_iota(jnp.int32, (G_CHUNK, G_CHUNK), 0)
        c = lax.broadcasted_iota(jnp.int32, (G_CHUNK, G_CHUNK), 1)
        causal = r >= c
        b_t = b_ref[...]
        for g in range(G_GROUPS):
            w = jnp.where(causal, w_ref[g], 0.0).astype(BF16)
            bias = b_t[:, g:g + 1]
            cols = slice(g * gd, (g + 1) * gd)
            for ci in range(tm // G_CHUNK):
                rows = slice(ci * G_CHUNK, (ci + 1) * G_CHUNK)
                mixed = _dot(w, v_ref[rows, cols]) + bias
                y_ref[rows, cols] = (u_ref[rows, cols] * mixed).astype(y_ref.dtype)

    return pl.pallas_call(
        body, out_shape=jax.ShapeDtypeStruct((n, width), BF16), grid=(n // tm,),
        in_specs=[pl.BlockSpec((tm, width), lambda i: (i, 0)),
                  pl.BlockSpec((tm, width), lambda i: (i, 0)),
                  pl.BlockSpec((None, G_GROUPS, G_CHUNK, G_CHUNK), lambda i: (layer, 0, 0, 0)),
                  pl.BlockSpec((None, G_CHUNK, G_GROUPS), lambda i: (layer, 0, 0))],
        out_specs=pl.BlockSpec((tm, width), lambda i: (i, 0)),
        compiler_params=_params("parallel"), name=name,
    )(u, v, w_s, b_s_t)


def _mlstm_scan(qk, v, o_gate, gcol, grow, norm_g, layer, *, batch, seq, chunk, name):
    n = batch * seq
    heads = M_HEADS
    dk = qk.shape[1] // (2 * heads)
    dv = v.shape[1] // heads
    nc = seq // chunk
    assert seq % chunk == 0
    k_scale = dk ** -0.5

    def body(q_ref, k_ref, v_ref, o_ref, gcol_ref, grow_ref, ng_ref, y_ref, state_ref, m_ref):
        @pl.when(pl.program_id(1) == 0)
        def _():
            state_ref[...] = jnp.zeros_like(state_ref)
            m_ref[...] = jnp.zeros_like(m_ref)

        gcol_v = gcol_ref[...]
        grow_v = grow_ref[...]
        r = lax.broadcasted_iota(jnp.int32, (chunk, chunk), 0)
        c = lax.broadcasted_iota(jnp.int32, (chunk, chunk), 1)
        causal = r >= c
        ones = jnp.ones((chunk, LANES), BF16)
        for h in range(heads):
            i_col = gcol_v[:, h:h + 1]
            b_col = gcol_v[:, heads + h:heads + h + 1]
            i_row = grow_v[h:h + 1, :]
            b_row = grow_v[heads + h:heads + h + 1, :]
            m_prev = m_ref[h][0:1, 0:1]
            q = q_ref[:, h * dk:(h + 1) * dk]
            k = k_ref[:, h * dk:(h + 1) * dk]
            vh = v_ref[:, h * dv:(h + 1) * dv]

            d = jnp.where(causal, b_col - b_row + i_row, -jnp.inf)
            m_inter = b_col + m_prev
            m_row = jnp.maximum(jnp.max(d, axis=-1, keepdims=True), m_inter)
            p = jnp.exp(d - m_row) * (_dot_nt(q, k) * k_scale)
            w_inter = jnp.exp(m_inter - m_row)
            q_state = _dot(q, state_ref[h].astype(BF16))
            num = _dot(p.astype(BF16), vh) + w_inter * q_state[:, :dv]
            den = jnp.sum(p, axis=-1, keepdims=True) + w_inter * q_state[:, dv:dv + 1]
            inv = 1.0 / jnp.maximum(jnp.abs(den), jnp.exp(-m_row))
            hn = _normalize(num * inv)
            cols = slice(h * dv, (h + 1) * dv)
            y_ref[:, cols] = (o_ref[:, cols] * (hn * ng_ref[:, cols])).astype(y_ref.dtype)

            b_last = b_col[chunk - 1:chunk, :]
            g = b_last - b_col + i_col
            m_new = jnp.maximum(b_last + m_prev, jnp.max(g, axis=0, keepdims=True))
            w_state = jnp.exp(g - m_new)
            decay = jnp.exp(b_last + m_prev - m_new)
            kw = (k.astype(F32) * (w_state * k_scale)).astype(BF16)
            v_ext = jnp.concatenate([vh, ones], axis=1)
            state_ref[h] = decay * state_ref[h] + _dot_tn(kw, v_ext)
            m_ref[h] = jnp.broadcast_to(m_new, (SUBLANES, LANES))

    return pl.pallas_call(
        body, out_shape=jax.ShapeDtypeStruct((n, heads * dv), BF16), grid=(batch, nc),
        in_specs=[pl.BlockSpec((chunk, heads * dk), lambda b, c: (b * nc + c, 0)),
                  pl.BlockSpec((chunk, heads * dk), lambda b, c: (b * nc + c, 1)),
                  pl.BlockSpec((chunk, heads * dv), lambda b, c: (b * nc + c, 0)),
                  pl.BlockSpec((chunk, heads * dv), lambda b, c: (b * nc + c, 0)),
                  pl.BlockSpec((chunk, LANES), lambda b, c: (b * nc + c, 0)),
                  pl.BlockSpec((2 * heads, chunk), lambda b, c: (0, b * nc + c)),
                  pl.BlockSpec((None, 1, heads * dv), lambda b, c: (layer, 0, 0))],
        out_specs=pl.BlockSpec((chunk, heads * dv), lambda b, c: (b * nc + c, 0)),
        scratch_shapes=[pltpu.VMEM((heads, dk, dv + LANES), F32),
                        pltpu.VMEM((heads, SUBLANES, LANES), F32)],
        compiler_params=_params("parallel", "arbitrary"), name=name,
    )(qk, qk, v, o_gate, gcol, grow, norm_g)


def _memory_attention(q, kv, *, batch, seq, mem_len, tq, name):
    n, width = q.shape
    hd = width // X_HEADS
    tq = _tile(seq, tq)
    nq = seq // tq
    scale = hd ** -0.5

    def body(q_ref, kv_ref, y_ref):
        for h in range(X_HEADS):
            cols = slice(h * hd, (h + 1) * hd)
            s = _dot_nt(q_ref[:, cols], kv_ref[:, cols]) * scale
            e = jnp.exp(s - jnp.max(s, axis=-1, keepdims=True))
            p = e * (1.0 / jnp.sum(e, axis=-1, keepdims=True))
            y_ref[:, cols] = _dot(p.astype(BF16), kv_ref[:, width + h * hd:width + (h + 1) * hd]
                                  ).astype(y_ref.dtype)

    return pl.pallas_call(
        body, out_shape=jax.ShapeDtypeStruct((n, width), BF16), grid=(batch, nq),
        in_specs=[pl.BlockSpec((tq, width), lambda b, i: (b * nq + i, 0)),
                  pl.BlockSpec((mem_len, 2 * width), lambda b, i: (b, 0))],
        out_specs=pl.BlockSpec((tq, width), lambda b, i: (b * nq + i, 0)),
        compiler_params=_params("parallel", "parallel"), name=name,
    )(q, kv)


def _gated_merge(x, y_a, y_b, y_c, w_g, b_g, w_pa, w_pb, w_pc, layer, *, tm, tn, name):
    n, d = x.shape
    tm = _tile(n, tm)
    nj = d // tn

    def body(x_ref, ya_ref, yb_ref, yc_ref, wg0, wg1, wg2, bg0, bg1, bg2, wa, wb, wc, o_ref):
        xv = x_ref[...]
        acc = _sigmoid(_dot(xv, wg0[...]) + bg0[...]) * _dot(ya_ref[...], wa[...])
        acc = acc + _sigmoid(_dot(xv, wg1[...]) + bg1[...]) * _dot(yb_ref[...], wb[...])
        acc = acc + _sigmoid(_dot(xv, wg2[...]) + bg2[...]) * _dot(yc_ref[...], wc[...])
        o_ref[...] = acc.astype(o_ref.dtype)

    def act_spec(arr):
        return pl.BlockSpec((tm, arr.shape[1]), lambda j, i: (i, 0))

    def gate_w_spec(k):
        return pl.BlockSpec((None, d, tn), lambda j, i: (layer, 0, k * nj + j))

    def gate_b_spec(k):
        return pl.BlockSpec((None, 1, tn), lambda j, i: (layer, 0, k * nj + j))

    def proj_spec(w):
        return pl.BlockSpec((None, w.shape[1], tn), lambda j, i: (layer, 0, j))

    return pl.pallas_call(
        body, out_shape=jax.ShapeDtypeStruct((n, d), BF16), grid=(nj, n // tm),
        in_specs=[act_spec(x), act_spec(y_a), act_spec(y_b), act_spec(y_c),
                  gate_w_spec(0), gate_w_spec(1), gate_w_spec(2),
                  gate_b_spec(0), gate_b_spec(1), gate_b_spec(2),
                  proj_spec(w_pa), proj_spec(w_pb), proj_spec(w_pc)],
        out_specs=pl.BlockSpec((tm, tn), lambda j, i: (i, j)),
        compiler_params=_params("parallel", "parallel"), name=name,
    )(x, y_a, y_b, y_c, w_g, w_g, w_g, b_g, b_g, b_g, w_pa, w_pb, w_pc)


def _residual_ln_outputs(pre, g_ref, b_ref, of_ref, ob_ref):
    y = _normalize(pre) * g_ref[...] + b_ref[...]
    of_ref[...] = y
    ob_ref[...] = y.astype(BF16)


def _proj_residual_ln(a, w, res, ln_g, ln_b, layer, alpha, *, tm, name):
    n, k = a.shape
    d = w.shape[2]
    tm = _tile(n, tm)

    def body(a_ref, w_ref, r_ref, g_ref, b_ref, of_ref, ob_ref):
        pre = alpha * r_ref[...] + _dot(a_ref[...], w_ref[...])
        _residual_ln_outputs(pre, g_ref, b_ref, of_ref, ob_ref)

    return pl.pallas_call(
        body,
        out_shape=[jax.ShapeDtypeStruct((n, d), F32), jax.ShapeDtypeStruct((n, d), BF16)],
        grid=(n // tm,),
        in_specs=[pl.BlockSpec((tm, k), lambda i: (i, 0)),
                  pl.BlockSpec((None, k, d), lambda i: (layer, 0, 0)),
                  pl.BlockSpec((tm, d), lambda i: (i, 0)),
                  pl.BlockSpec((None, 1, d), lambda i: (layer, 0, 0)),
                  pl.BlockSpec((None, 1, d), lambda i: (layer, 0, 0))],
        out_specs=[pl.BlockSpec((tm, d), lambda i: (i, 0)), pl.BlockSpec((tm, d), lambda i: (i, 0))],
        compiler_params=_params("parallel"), name=name,
    )(a, w, res, ln_g, ln_b)


def _ffn_up(x, w_gu, layer, d_ff, *, tm, tn, name):
    n, d = x.shape
    tm = _tile(n, tm)
    nj = d_ff // tn
    assert d_ff % tn == 0

    def body(x_ref, wg_ref, wu_ref, o_ref):
        xv = x_ref[...]
        o_ref[...] = (_silu(_dot(xv, wg_ref[...])) * _dot(xv, wu_ref[...])).astype(o_ref.dtype)

    return pl.pallas_call(
        body, out_shape=jax.ShapeDtypeStruct((n, d_ff), BF16), grid=(nj, n // tm),
        in_specs=[pl.BlockSpec((tm, d), lambda j, i: (i, 0)),
                  pl.BlockSpec((None, d, tn), lambda j, i: (layer, 0, j)),
                  pl.BlockSpec((None, d, tn), lambda j, i: (layer, 0, nj + j))],
        out_specs=pl.BlockSpec((tm, tn), lambda j, i: (i, j)),
        compiler_params=_params("parallel", "parallel"), name=name,
    )(x, w_gu, w_gu)


def _ffn_down_residual_ln(h, w, res, ln_g, ln_b, layer, alpha, *, tm, tk, name):
    n, kdim = h.shape
    d = w.shape[2]
    tm = _tile(n, tm)
    assert kdim % tk == 0
    nk = kdim // tk

    def body(h_ref, w_ref, r_ref, g_ref, b_ref, of_ref, ob_ref, acc_ref):
        kk = pl.program_id(1)

        @pl.when(kk == 0)
        def _():
            acc_ref[...] = alpha * r_ref[...]

        acc_ref[...] += _dot(h_ref[...], w_ref[...])

        @pl.when(kk == nk - 1)
        def _():
            _residual_ln_outputs(acc_ref[...], g_ref, b_ref, of_ref, ob_ref)

    return pl.pallas_call(
        body,
        out_shape=[jax.ShapeDtypeStruct((n, d), F32), jax.ShapeDtypeStruct((n, d), BF16)],
        grid=(n // tm, nk),
        in_specs=[pl.BlockSpec((tm, tk), lambda i, kk: (i, kk)),
                  pl.BlockSpec((None, tk, d), lambda i, kk: (layer, kk, 0)),
                  pl.BlockSpec((tm, d), lambda i, kk: (i, 0)),
                  pl.BlockSpec((None, 1, d), lambda i, kk: (layer, 0, 0)),
                  pl.BlockSpec((None, 1, d), lambda i, kk: (layer, 0, 0))],
        out_specs=[pl.BlockSpec((tm, d), lambda i, kk: (i, 0)),
                   pl.BlockSpec((tm, d), lambda i, kk: (i, 0))],
        scratch_shapes=[pltpu.VMEM((tm, d), F32)],
        compiler_params=_params("parallel", "arbitrary"), name=name,
    )(h, w, res, ln_g, ln_b)


def kernel(x, mem, mem_ln_g, mem_ln_b, w_in, b_in, g_ln_g, g_ln_b, g_ws, g_bs, m_conv_w, m_conv_b,
           m_norm_g, x_w_kv, w_pa, w_pb, w_pc, w_out, ln1_g, ln1_b, w_gu, w_down, ln2_g, ln2_b):
    batch, seq, d = x.shape
    depth = w_in.shape[0]
    mem_len = mem.shape[1]
    n = batch * seq
    g_width = g_ws.shape[1] * (d // 16)
    qk_width = m_conv_w.shape[2]
    v_width = m_norm_g.shape[1]
    x_width = x_w_kv.shape[2] // 2
    d_ff = w_down.shape[1]
    alpha = (2 * depth) ** 0.25
    heads = M_HEADS

    off_u, off_v = 0, g_width
    off_qk = 2 * g_width
    off_vm = off_qk + qk_width
    off_o = off_vm + v_width
    off_i = off_o + v_width
    off_qx = off_i + 2 * heads
    off_g = off_qx + x_width
    assert w_in.shape[2] == off_g + N_BRANCH * d
    chunk = min(M_SCAN_CHUNK, seq)
    tiles_per_seq_1024 = None

    w_main = w_in[:, :, :off_i].astype(BF16)
    b_main = b_in[:, None, :off_i]
    w_if = w_in[:, :, off_i:off_qx]
    w_if_col = jnp.pad(w_if, ((0, 0), (0, 0), (0, LANES - 2 * heads))).astype(BF16)
    w_if_row = jnp.pad(jnp.swapaxes(w_if, 1, 2), ((0, 0), (0, 16 - 2 * heads), (0, 0))).astype(BF16)
    b_if = b_in[:, off_i:off_qx]
    b_if_col = jnp.pad(b_if, ((0, 0), (0, LANES - 2 * heads)))[:, None, :]
    b_if_row = b_if[:, :, None]
    w_qx = w_in[:, :, off_qx:off_g].astype(BF16)
    b_qx = b_in[:, None, off_qx:off_g]
    w_gate = w_in[:, :, off_g:].astype(BF16)
    b_gate = b_in[:, None, off_g:]
    w_kv = x_w_kv.astype(BF16)
    w_pa_b, w_pb_b, w_pc_b = w_pa.astype(BF16), w_pb.astype(BF16), w_pc.astype(BF16)
    w_out_b, w_gu_b, w_down_b = w_out.astype(BF16), w_gu.astype(BF16), w_down.astype(BF16)
    g_bs_t = jnp.swapaxes(g_bs, 1, 2)
    row3 = lambda a: a[:, None, :]
    g_ln_g3, g_ln_b3 = row3(g_ln_g), row3(g_ln_b)
    conv_b3, norm_g3 = row3(m_conv_b), row3(m_norm_g)
    ln1_g3, ln1_b3, ln2_g3, ln2_b3 = row3(ln1_g), row3(ln1_b), row3(ln2_g), row3(ln2_b)

    mem_n = _layernorm_rows(mem.reshape(batch * mem_len, d), mem_ln_g, mem_ln_b,
                            tm=256, out_dtype=BF16, name="mem_ln")
    xf = x.reshape(n, d)
    xb = xf.astype(BF16)

    tm_big = _tile(seq, 1024)
    for l in range(depth):
        lin = functools.partial(_linear, xb, layer=l, tm=tm_big)
        u_act, = lin(w_main, b_main, col0=off_u, ncols=g_width, tn=1024,
                     epilogue=functools.partial(_epi_act, _gelu_tanh),
                     outs=[(g_width, 1024, F32)], name=f"l{l}_u")
        v_ln, = lin(w_main, b_main, col0=off_v, ncols=g_width, tn=g_width, epilogue=_epi_gelu_ln,
                    outs=[(g_width, g_width, BF16)],
                    extras=[(g_ln_g3, pl.BlockSpec((None, 1, g_width), lambda j, i: (l, 0, 0))),
                            (g_ln_b3, pl.BlockSpec((None, 1, g_width), lambda j, i: (l, 0, 0)))],
                    name=f"l{l}_v")
        qk, = lin(w_main, b_main, col0=off_qk, ncols=qk_width, tn=1024,
                  epilogue=functools.partial(_epi_conv_silu, seq // tm_big),
                  outs=[(qk_width, 1024, BF16)],
                  extras=[(m_conv_w, pl.BlockSpec((None, M_CONV, 1024), lambda j, i: (l, 0, j))),
                          (conv_b3, pl.BlockSpec((None, 1, 1024), lambda j, i: (l, 0, j)))],
                  scratch=[pltpu.VMEM((SUBLANES, 1024), F32)], row_sem="arbitrary",
                  name=f"l{l}_qk")
        v_m, = lin(w_main, b_main, col0=off_vm, ncols=v_width, tn=1024,
                   epilogue=functools.partial(_epi_act, lambda a: a),
                   outs=[(v_width, 1024, BF16)], name=f"l{l}_vm")
        o_gate, = lin(w_main, b_main, col0=off_o, ncols=v_width, tn=1024,
                      epilogue=functools.partial(_epi_act, _sigmoid),
                      outs=[(v_width, 1024, F32)], name=f"l{l}_o")
        q_x, = lin(w_qx, b_qx, col0=0, ncols=x_width, tn=1024,
                   epilogue=functools.partial(_epi_act, lambda a: a),
                   outs=[(x_width, 1024, BF16)], name=f"l{l}_qx")
        gcol, grow = _mlstm_gates(xb, w_if_col, w_if_row, b_if_col, b_if_row, l,
                                  tm=tm_big, chunk=chunk, name=f"l{l}_if")

        y_a = _spatial_gating(u_act, v_ln, g_ws, g_bs_t, l, tm=512, name=f"l{l}_sgu")
        y_b = _mlstm_scan(qk, v_m, o_gate, gcol, grow, norm_g3, l, batch=batch, seq=seq,
                          chunk=chunk, name=f"l{l}_mlstm")
        kv, = _linear(mem_n, w_kv, None, l, 0, 2 * x_width, tm=batch * mem_len, tn=1024,
                      epilogue=functools.partial(_epi_act, lambda a: a),
                      outs=[(2 * x_width, 1024, BF16)], name=f"l{l}_kv")
        y_c = _memory_attention(q_x, kv, batch=batch, seq=seq, mem_len=mem_len, tq=512,
                                name=f"l{l}_xattn")

        merged = _gated_merge(xb, y_a, y_b, y_c, w_gate, b_gate, w_pa_b, w_pb_b, w_pc_b, l,
                              tm=512, tn=512, name=f"l{l}_merge")
        xf, xb = _proj_residual_ln(merged, w_out_b, xf, ln1_g3, ln1_b3, l, alpha, tm=512,
                                   name=f"l{l}_out_ln1")
        hidden = _ffn_up(xb, w_gu_b, l, d_ff, tm=tm_big, tn=512, name=f"l{l}_ffn_up")
        xf, xb = _ffn_down_residual_ln(hidden, w_down_b, xf, ln2_g3, ln2_b3, l, alpha,
                                       tm=512, tk=512, name=f"l{l}_ffn_down_ln2")
    return xf.reshape(batch, seq, d)
```

```python
import functools
import math

import jax
import jax.numpy as jnp
from jax import lax
from jax.experimental import pallas as pl
from jax.experimental.pallas import tpu as pltpu

F32 = jnp.float32
BF16 = jnp.bfloat16

LN_EPS = 1e-5
LANES = 128
SUBLANES = 8
VMEM_LIMIT_BYTES = 56 * 1024 * 1024

G_CHUNK = 128
G_GROUPS = 8
M_HEADS = 4
M_CONV = 4
M_SCAN_CHUNK = 256
X_HEADS = 4
N_BRANCH = 3


def _params(*sem):
    return pltpu.CompilerParams(dimension_semantics=sem, vmem_limit_bytes=VMEM_LIMIT_BYTES)


def _tile(n, pref):
    t = min(n, pref)
    assert n % t == 0, (n, pref)
    return t


def _sigmoid(x):
    return 1.0 / (1.0 + jnp.exp(-x))


def _silu(x):
    return x * _sigmoid(x)


def _gelu_tanh(x):
    c = math.sqrt(2.0 / math.pi)
    return x * (0.5 * (1.0 + jnp.tanh(c * (x + 0.044715 * (x * x * x)))))


def _log_sigmoid(x):
    return jnp.minimum(x, 0.0) - jnp.log1p(jnp.exp(-jnp.abs(x)))


def _normalize(x):
    mu = jnp.mean(x, axis=-1, keepdims=True)
    xc = x - mu
    var = jnp.mean(xc * xc, axis=-1, keepdims=True)
    return xc * lax.rsqrt(var + LN_EPS)


def _dot(a, b):
    return jnp.dot(a, b, preferred_element_type=F32)


def _dot_nt(a, b):
    return lax.dot_general(a, b, (((1,), (1,)), ((), ())), preferred_element_type=F32)


def _dot_tn(a, b):
    return lax.dot_general(a, b, (((0,), (0,)), ((), ())), preferred_element_type=F32)


def _cast_weight_once(w_ref, wb_ref, row_axis):
    @pl.when(pl.program_id(row_axis) == 0)
    def _():
        wb_ref[...] = w_ref[...].astype(BF16)


def _linear(x, w, bias, layer, col0, ncols, *, tm, tn, epilogue, outs, extras=(), scratch=(),
            sub=None, row_sub=None, prologue=None, name):
    n, k = x.shape
    tm = _tile(n, tm)
    sub = tn if sub is None else sub
    row_sub = tm if row_sub is None else min(row_sub, tm)
    assert ncols % tn == 0 and col0 % tn == 0 and tn % sub == 0 and tm % row_sub == 0
    j0 = col0 // tn
    grid = (ncols // tn, n // tm)
    has_bias = bias is not None
    cast_w = w.dtype != BF16
    in_specs = [pl.BlockSpec((tm, k), lambda j, i: (i, 0)),
                pl.BlockSpec((None, k, tn), lambda j, i: (layer, 0, j0 + j))]
    args = [x, w]
    if has_bias:
        in_specs.append(pl.BlockSpec((None, 1, tn), lambda j, i: (layer, 0, j0 + j)))
        args.append(bias)
    for arr, spec in extras:
        in_specs.append(spec)
        args.append(arr)
    out_shape = [jax.ShapeDtypeStruct((n, tot), dt) for tot, dt in outs]
    out_specs = [pl.BlockSpec((tm, tn), lambda j, i: (i, j)) for _ in outs]
    n_extra, n_out = len(extras), len(outs)
    scratch = list(scratch) + ([pltpu.VMEM((k, tn), BF16)] if cast_w else [])

    def body(*refs):
        x_ref, w_ref = refs[0], refs[1]
        pos = 2
        b_ref = None
        if has_bias:
            b_ref = refs[pos]
            pos += 1
        extra_refs = refs[pos:pos + n_extra]
        out_refs = refs[pos + n_extra:pos + n_extra + n_out]
        scratch_refs = refs[pos + n_extra + n_out:]
        wb_ref = w_ref
        if cast_w:
            wb_ref, scratch_refs = scratch_refs[-1], scratch_refs[:-1]
            _cast_weight_once(w_ref, wb_ref, 1)
        if prologue is not None:
            prologue(scratch_refs)
        for rb in range(tm // row_sub):
            rows = slice(rb * row_sub, (rb + 1) * row_sub)
            xv = x_ref[rows, :]
            for sb in range(tn // sub):
                cols = slice(sb * sub, (sb + 1) * sub)
                acc = _dot(xv, wb_ref[:, cols])
                if has_bias:
                    acc = acc + b_ref[:, cols]
                epilogue(acc, rows, cols, extra_refs, out_refs, scratch_refs)

    res = pl.pallas_call(
        body, out_shape=out_shape, grid=grid, in_specs=in_specs, out_specs=out_specs,
        scratch_shapes=scratch, compiler_params=_params("parallel", "arbitrary"), name=name,
    )(*args)
    return res


def _epi_act(act, acc, rows, cols, extra_refs, out_refs, scratch_refs):
    out_refs[0][rows, cols] = act(acc).astype(out_refs[0].dtype)


def _epi_gelu_ln(acc, rows, cols, extra_refs, out_refs, scratch_refs):
    g_ref, b_ref = extra_refs
    y = _normalize(_gelu_tanh(acc)) * g_ref[...] + b_ref[...]
    out_refs[0][rows, :] = y.astype(out_refs[0].dtype)


def _conv_carry_reset(tiles_per_seq, scratch_refs):
    carry_ref, = scratch_refs

    @pl.when(pl.program_id(1) % tiles_per_seq == 0)
    def _():
        carry_ref[...] = jnp.zeros_like(carry_ref)


def _epi_conv_silu(acc, rows, cols, extra_refs, out_refs, scratch_refs):
    cw_ref, cb_ref = extra_refs
    carry_ref, = scratch_refs
    tm = acc.shape[0]
    carry = carry_ref[:, cols]
    cw = cw_ref[:, cols]
    y = acc * cw[M_CONV - 1:M_CONV, :] + cb_ref[:, cols]
    top_rows = lax.broadcasted_iota(jnp.int32, (SUBLANES, acc.shape[1]), 0)
    for s in range(1, M_CONV):
        shifted = pltpu.roll(acc, s, axis=0)
        top = jnp.where(top_rows < s, pltpu.roll(carry, s, axis=0), shifted[:SUBLANES])
        shifted = jnp.concatenate([top, shifted[SUBLANES:]], axis=0)
        y = y + shifted * cw[M_CONV - 1 - s:M_CONV - s, :]
    carry_ref[:, cols] = acc[tm - SUBLANES:, :]
    out_refs[0][rows, cols] = _silu(y).astype(out_refs[0].dtype)


def _layernorm_rows(x, g, b, *, tm, out_dtype, name):
    n, d = x.shape
    tm = _tile(n, tm)

    def body(x_ref, g_ref, b_ref, o_ref):
        o_ref[...] = (_normalize(x_ref[...]) * g_ref[...] + b_ref[...]).astype(o_ref.dtype)

    return pl.pallas_call(
        body, out_shape=jax.ShapeDtypeStruct((n, d), out_dtype), grid=(n // tm,),
        in_specs=[pl.BlockSpec((tm, d), lambda i: (i, 0)),
                  pl.BlockSpec((1, d), lambda i: (0, 0)),
                  pl.BlockSpec((1, d), lambda i: (0, 0))],
        out_specs=pl.BlockSpec((tm, d), lambda i: (i, 0)),
        compiler_params=_params("parallel"), name=name,
    )(x, g.reshape(1, d), b.reshape(1, d))


def _split3_bf16(x):
    h1 = x.astype(BF16)
    r1 = x - h1.astype(F32)
    h2 = r1.astype(BF16)
    r2 = r1 - h2.astype(F32)
    return h1, h2, r2.astype(BF16)


def _mlstm_gates(x, w_col, w_row, b_col, b_row, layer, *, tm, chunk, name):
    n, k = x.shape
    tm = _tile(n, tm)
    assert tm % chunk == 0
    heads = M_HEADS
    rep_w = 2 * heads * LANES

    def body(x_ref, wc_ref, wr_ref, bc_ref, br_ref, gcol_ref, grow_ref):
        xv = x_ref[...]
        zc = _dot(xv, wc_ref[...]) + bc_ref[...]
        zr = _dot_nt(wr_ref[...], xv)[:2 * heads] + br_ref[...]
        lfc = _log_sigmoid(zc)
        lfr = _log_sigmoid(zr)
        r = lax.broadcasted_iota(jnp.int32, (chunk, chunk), 0)
        c = lax.broadcasted_iota(jnp.int32, (chunk, chunk), 1)
        tril = jnp.where(r >= c, 1.0, 0.0).astype(BF16)
        triu = jnp.where(r <= c, 1.0, 0.0).astype(BF16)
        lane = lax.broadcasted_iota(jnp.int32, (chunk, LANES), 1)
        is_f_lane = (lane >= heads) & (lane < 2 * heads)
        sub = lax.broadcasted_iota(jnp.int32, (2 * heads, chunk), 0)
        src = lax.broadcasted_iota(jnp.int32, (LANES, rep_w), 0)
        dst = lax.broadcasted_iota(jnp.int32, (LANES, rep_w), 1)
        replicate = jnp.where((dst >= src * LANES) & (dst < (src + 1) * LANES), 1.0, 0.0
                              ).astype(BF16)
        for ci in range(tm // chunk):
            sl = slice(ci * chunk, (ci + 1) * chunk)
            a1, a2, a3 = _split3_bf16(lfc[sl])
            cs = _dot(tril, a1) + _dot(tril, a2) + _dot(tril, a3)
            g1, g2, g3 = _split3_bf16(jnp.where(is_f_lane, cs, zc[sl]))
            gcol_ref[sl, :] = _dot(g1, replicate) + _dot(g2, replicate) + _dot(g3, replicate)
            r1, r2, r3 = _split3_bf16(lfr[:, sl])
            rs = _dot(r1, triu) + _dot(r2, triu) + _dot(r3, triu)
            grow_ref[:, sl] = jnp.where(sub >= heads, rs, zr[:, sl])

    return pl.pallas_call(
        body,
        out_shape=[jax.ShapeDtypeStruct((n, rep_w), F32), jax.ShapeDtypeStruct((2 * heads, n), F32)],
        grid=(n // tm,),
        in_specs=[pl.BlockSpec((tm, k), lambda i: (i, 0)),
                  pl.BlockSpec((None, k, LANES), lambda i: (layer, 0, 0)),
                  pl.BlockSpec((None, 16, k), lambda i: (layer, 0, 0)),
                  pl.BlockSpec((None, 1, LANES), lambda i: (layer, 0, 0)),
                  pl.BlockSpec((None, 2 * heads, 1), lambda i: (layer, 0, 0))],
        out_specs=[pl.BlockSpec((tm, rep_w), lambda i: (i, 0)),
                   pl.BlockSpec((2 * heads, tm), lambda i: (0, i))],
        compiler_params=_params("parallel"), name=name,
    )(x, w_col, w_row, b_col, b_row)


def _spatial_gating(u, v, w_s, b_s_t, layer, *, tm, name):
    n, width = u.shape
    tm = _tile(n, tm)
    gd = width // G_GROUPS
    assert tm % G_CHUNK == 0 and gd % LANES == 0

    def body(u_ref, v_ref, w_ref, b_ref, y_ref):
        r = lax.broadcasted_iota(jnp.int32, (G_CHUNK, G_CHUNK), 0)
        c = lax.broadcasted_iota(jnp.int32, (G_CHUNK, G_CHUNK), 1)
        causal = r >= c
        b_t = b_ref[...]
        for g in range(G_GROUPS):
            w = jnp.where(causal, w_ref[g], 0.0).astype(BF16)
            bias = b_t[:, g:g + 1]
            cols = slice(g * gd, (g + 1) * gd)
            for ci in range(tm // G_CHUNK):
                rows = slice(ci * G_CHUNK, (ci + 1) * G_CHUNK)
                mixed = _dot(w, v_ref[rows, cols]) + bias
                y_ref[rows, cols] = (u_ref[rows, cols] * mixed).astype(y_ref.dtype)

    return pl.pallas_call(
        body, out_shape=jax.ShapeDtypeStruct((n, width), BF16), grid=(n // tm,),
        in_specs=[pl.BlockSpec((tm, width), lambda i: (i, 0)),
                  pl.BlockSpec((tm, width), lambda i: (i, 0)),
                  pl.BlockSpec((None, G_GROUPS, G_CHUNK, G_CHUNK), lambda i: (layer, 0, 0, 0)),
                  pl.BlockSpec((None, G_CHUNK, G_GROUPS), lambda i: (layer, 0, 0))],
        out_specs=pl.BlockSpec((tm, width), lambda i: (i, 0)),
        compiler_params=_params("parallel"), name=name,
    )(u, v, w_s, b_s_t)


def _lane_blocks(a):
    return [a[:, j * LANES:(j + 1) * LANES] for j in range(a.shape[1] // LANES)]


def _scale_rows(a, r):
    return jnp.concatenate([blk * r for blk in _lane_blocks(a)], axis=1)


def _mlstm_scan(qk, v, o_gate, gcol, grow, norm_g, layer, *, batch, seq, chunk, group, name):
    n = batch * seq
    heads = M_HEADS
    dk = qk.shape[1] // (2 * heads)
    dv = v.shape[1] // heads
    nc = seq // chunk
    assert seq % chunk == 0
    k_scale = dk ** -0.5

    def body(q_ref, k_ref, v_ref, o_ref, gcol_ref, grow_ref, ng_ref, y_ref, state_ref, m_ref):
        @pl.when(pl.program_id(1) == 0)
        def _():
            state_ref[...] = jnp.zeros_like(state_ref)
            m_ref[...] = jnp.zeros_like(m_ref)

        grow_v = grow_ref[...]
        r = lax.broadcasted_iota(jnp.int32, (chunk, LANES), 0)
        c = lax.broadcasted_iota(jnp.int32, (chunk, LANES), 1)
        causal = [r >= c + j * LANES for j in range(chunk // LANES)]
        ones = jnp.ones((chunk, LANES), BF16)

        def head_group(hs):
            i_col = {h: gcol_ref[:, h * LANES:(h + 1) * LANES] for h in hs}
            b_col = {h: gcol_ref[:, (heads + h) * LANES:(heads + h + 1) * LANES] for h in hs}
            src_row = {h: _lane_blocks(grow_v[h:h + 1, :] - grow_v[heads + h:heads + h + 1, :])
                       for h in hs}
            m_prev = {h: m_ref[h][0:1, :] for h in hs}
            q = {h: q_ref[:, h * dk:(h + 1) * dk] for h in hs}
            k = {h: k_ref[:, h * dk:(h + 1) * dk] for h in hs}
            vh = {h: v_ref[:, h * dv:(h + 1) * dv] for h in hs}

            scores = {h: _lane_blocks(_dot_nt(q[h], k[h])) for h in hs}
            q_state = {h: _dot(q[h], state_ref[h].astype(BF16)) for h in hs}
            d = {h: [jnp.where(causal[j], b_col[h] + src_row[h][j], -jnp.inf)
                     for j in range(len(causal))] for h in hs}
            m_inter = {h: b_col[h] + m_prev[h] for h in hs}
            m_row = {h: jnp.maximum(
                jnp.max(functools.reduce(jnp.maximum, d[h]), axis=-1, keepdims=True), m_inter[h])
                for h in hs}
            p = {h: jnp.concatenate(
                [jnp.exp(d[h][j] - m_row[h]) * (scores[h][j] * k_scale) for j in range(len(causal))],
                axis=1) for h in hs}
            w_inter = {h: jnp.exp(m_inter[h] - m_row[h]) for h in hs}
            pv = {h: _dot(p[h].astype(BF16), vh[h]) for h in hs}
            den = {h: jnp.sum(p[h], axis=-1, keepdims=True) + w_inter[h] * q_state[h][:, dv:]
                   for h in hs}
            inv = {h: 1.0 / jnp.maximum(jnp.abs(den[h]), jnp.exp(-m_row[h])) for h in hs}
            hn = {h: _normalize(_scale_rows(pv[h] + _scale_rows(q_state[h][:, :dv], w_inter[h]),
                                            inv[h])) for h in hs}
            for h in hs:
                cols = slice(h * dv, (h + 1) * dv)
                y_ref[:, cols] = (o_ref[:, cols] * (hn[h] * ng_ref[:, cols])).astype(y_ref.dtype)

            b_last = {h: b_col[h][chunk - 1:chunk, :] for h in hs}
            g = {h: b_last[h] - b_col[h] + i_col[h] for h in hs}
            m_new = {h: jnp.maximum(b_last[h] + m_prev[h], jnp.max(g[h], axis=0, keepdims=True))
                     for h in hs}
            w_state = {h: jnp.exp(g[h] - m_new[h]) * k_scale for h in hs}
            decay = {h: jnp.exp(b_last[h] + m_prev[h] - m_new[h]) for h in hs}
            kw = {h: _scale_rows(k[h].astype(F32), w_state[h]).astype(BF16) for h in hs}
            upd = {h: _dot_tn(kw[h], jnp.concatenate([vh[h], ones], axis=1)) for h in hs}
            for h in hs:
                state_ref[h] = decay[h][:, :1] * state_ref[h] + upd[h]
                m_ref[h] = jnp.broadcast_to(m_new[h], (SUBLANES, LANES))

        for h0 in range(0, heads, group):
            head_group(range(h0, min(h0 + group, heads)))

    return pl.pallas_call(
        body, out_shape=jax.ShapeDtypeStruct((n, heads * dv), BF16), grid=(batch, nc),
        in_specs=[pl.BlockSpec((chunk, heads * dk), lambda b, c: (b * nc + c, 0)),
                  pl.BlockSpec((chunk, heads * dk), lambda b, c: (b * nc + c, 1)),
                  pl.BlockSpec((chunk, heads * dv), lambda b, c: (b * nc + c, 0)),
                  pl.BlockSpec((chunk, heads * dv), lambda b, c: (b * nc + c, 0)),
                  pl.BlockSpec((chunk, 2 * heads * LANES), lambda b, c: (b * nc + c, 0)),
                  pl.BlockSpec((2 * heads, chunk), lambda b, c: (0, b * nc + c)),
                  pl.BlockSpec((None, 1, heads * dv), lambda b, c: (layer, 0, 0))],
        out_specs=pl.BlockSpec((chunk, heads * dv), lambda b, c: (b * nc + c, 0)),
        scratch_shapes=[pltpu.VMEM((heads, dk, dv + LANES), F32),
                        pltpu.VMEM((heads, SUBLANES, LANES), F32)],
        compiler_params=_params("parallel", "arbitrary"), name=name,
    )(qk, qk, v, o_gate, gcol, grow, norm_g)


def _memory_attention(q, kv, *, batch, seq, mem_len, tq, name):
    n, width = q.shape
    hd = width // X_HEADS
    tq = _tile(seq, tq)
    nq = seq // tq
    scale = hd ** -0.5

    def body(q_ref, kv_ref, y_ref):
        for h in range(X_HEADS):
            cols = slice(h * hd, (h + 1) * hd)
            s = _dot_nt(q_ref[:, cols], kv_ref[:, cols]) * scale
            e = jnp.exp(s - jnp.max(s, axis=-1, keepdims=True))
            p = e * (1.0 / jnp.sum(e, axis=-1, keepdims=True))
            y_ref[:, cols] = _dot(p.astype(BF16), kv_ref[:, width + h * hd:width + (h + 1) * hd]
                                  ).astype(y_ref.dtype)

    return pl.pallas_call(
        body, out_shape=jax.ShapeDtypeStruct((n, width), BF16), grid=(batch, nq),
        in_specs=[pl.BlockSpec((tq, width), lambda b, i: (b * nq + i, 0)),
                  pl.BlockSpec((mem_len, 2 * width), lambda b, i: (b, 0))],
        out_specs=pl.BlockSpec((tq, width), lambda b, i: (b * nq + i, 0)),
        compiler_params=_params("parallel", "parallel"), name=name,
    )(q, kv)


def _gated_merge(x, y_a, y_b, y_c, w_g, b_g, w_pa, w_pb, w_pc, layer, *, tm, tn, name):
    n, d = x.shape
    tm = _tile(n, tm)
    nj = d // tn

    def body(x_ref, ya_ref, yb_ref, yc_ref, wg0, wg1, wg2, bg0, bg1, bg2, wa, wb, wc, o_ref,
             wa_b, wb_b, wc_b):
        for w_ref, wb_ref in ((wa, wa_b), (wb, wb_b), (wc, wc_b)):
            _cast_weight_once(w_ref, wb_ref, 1)
        xv = x_ref[...]
        acc = _sigmoid(_dot(xv, wg0[...]) + bg0[...]) * _dot(ya_ref[...], wa_b[...])
        acc = acc + _sigmoid(_dot(xv, wg1[...]) + bg1[...]) * _dot(yb_ref[...], wb_b[...])
        acc = acc + _sigmoid(_dot(xv, wg2[...]) + bg2[...]) * _dot(yc_ref[...], wc_b[...])
        o_ref[...] = acc.astype(o_ref.dtype)

    def act_spec(arr):
        return pl.BlockSpec((tm, arr.shape[1]), lambda j, i: (i, 0))

    def gate_w_spec(k):
        return pl.BlockSpec((None, d, tn), lambda j, i: (layer, 0, k * nj + j))

    def gate_b_spec(k):
        return pl.BlockSpec((None, 1, tn), lambda j, i: (layer, 0, k * nj + j))

    def proj_spec(w):
        return pl.BlockSpec((None, w.shape[1], tn), lambda j, i: (layer, 0, j))

    return pl.pallas_call(
        body, out_shape=jax.ShapeDtypeStruct((n, d), BF16), grid=(nj, n // tm),
        in_specs=[act_spec(x), act_spec(y_a), act_spec(y_b), act_spec(y_c),
                  gate_w_spec(0), gate_w_spec(1), gate_w_spec(2),
                  gate_b_spec(0), gate_b_spec(1), gate_b_spec(2),
                  proj_spec(w_pa), proj_spec(w_pb), proj_spec(w_pc)],
        out_specs=pl.BlockSpec((tm, tn), lambda j, i: (i, j)),
        scratch_shapes=[pltpu.VMEM((w.shape[1], tn), BF16) for w in (w_pa, w_pb, w_pc)],
        compiler_params=_params("parallel", "arbitrary"), name=name,
    )(x, y_a, y_b, y_c, w_g, w_g, w_g, b_g, b_g, b_g, w_pa, w_pb, w_pc)


def _proj_residual_ln(a, w, res, ln_g, ln_b, layer, alpha, *, tm, row_sub, name):
    n, k = a.shape
    d = w.shape[2]
    tm = _tile(n, tm)
    row_sub = min(row_sub, tm)
    assert tm % row_sub == 0

    def body(a_ref, w_ref, r_ref, g_ref, b_ref, of_ref, ob_ref):
        for rb in range(tm // row_sub):
            rows = slice(rb * row_sub, (rb + 1) * row_sub)
            pre = alpha * r_ref[rows, :] + _dot(a_ref[rows, :], w_ref[...])
            y = _normalize(pre) * g_ref[...] + b_ref[...]
            of_ref[rows, :] = y
            ob_ref[rows, :] = y.astype(BF16)

    return pl.pallas_call(
        body,
        out_shape=[jax.ShapeDtypeStruct((n, d), F32), jax.ShapeDtypeStruct((n, d), BF16)],
        grid=(n // tm,),
        in_specs=[pl.BlockSpec((tm, k), lambda i: (i, 0)),
                  pl.BlockSpec((None, k, d), lambda i: (layer, 0, 0), pipeline_mode=pl.Buffered(1)),
                  pl.BlockSpec((tm, d), lambda i: (i, 0)),
                  pl.BlockSpec((None, 1, d), lambda i: (layer, 0, 0)),
                  pl.BlockSpec((None, 1, d), lambda i: (layer, 0, 0))],
        out_specs=[pl.BlockSpec((tm, d), lambda i: (i, 0)), pl.BlockSpec((tm, d), lambda i: (i, 0))],
        compiler_params=_params("parallel"), name=name,
    )(a, w, res, ln_g, ln_b)


def _ffn_up(x, w_gu, layer, d_ff, *, tm, tn, sub, name):
    n, d = x.shape
    tm = _tile(n, tm)
    nj = d_ff // tn
    assert d_ff % tn == 0 and tn % sub == 0

    def body(x_ref, wg_ref, wu_ref, o_ref, wg_b, wu_b):
        _cast_weight_once(wg_ref, wg_b, 1)
        _cast_weight_once(wu_ref, wu_b, 1)
        xv = x_ref[...]
        for sb in range(tn // sub):
            cols = slice(sb * sub, (sb + 1) * sub)
            o_ref[:, cols] = (_silu(_dot(xv, wg_b[:, cols])) * _dot(xv, wu_b[:, cols])
                              ).astype(o_ref.dtype)

    return pl.pallas_call(
        body, out_shape=jax.ShapeDtypeStruct((n, d_ff), BF16), grid=(nj, n // tm),
        in_specs=[pl.BlockSpec((tm, d), lambda j, i: (i, 0)),
                  pl.BlockSpec((None, d, tn), lambda j, i: (layer, 0, j)),
                  pl.BlockSpec((None, d, tn), lambda j, i: (layer, 0, nj + j))],
        out_specs=pl.BlockSpec((tm, tn), lambda j, i: (i, j)),
        scratch_shapes=[pltpu.VMEM((d, tn), BF16), pltpu.VMEM((d, tn), BF16)],
        compiler_params=_params("parallel", "arbitrary"), name=name,
    )(x, w_gu, w_gu)


def kernel(x, mem, mem_ln_g, mem_ln_b, w_in, b_in, g_ln_g, g_ln_b, g_ws, g_bs, m_conv_w, m_conv_b,
           m_norm_g, x_w_kv, w_pa, w_pb, w_pc, w_out, ln1_g, ln1_b, w_gu, w_down, ln2_g, ln2_b):
    batch, seq, d = x.shape
    depth = w_in.shape[0]
    mem_len = mem.shape[1]
    n = batch * seq
    g_width = g_ws.shape[1] * (d // 16)
    qk_width = m_conv_w.shape[2]
    v_width = m_norm_g.shape[1]
    x_width = x_w_kv.shape[2] // 2
    d_ff = w_down.shape[1]
    alpha = (2 * depth) ** 0.25
    heads = M_HEADS

    off_u, off_v = 0, g_width
    off_qk = 2 * g_width
    off_vm = off_qk + qk_width
    off_o = off_vm + v_width
    off_i = off_o + v_width
    off_qx = off_i + 2 * heads
    off_g = off_qx + x_width
    assert w_in.shape[2] == off_g + N_BRANCH * d
    chunk = min(M_SCAN_CHUNK, seq)

    b_in3 = b_in[:, None, :]
    w_if = w_in[:, :, off_i:off_qx]
    w_if_col = jnp.pad(w_if, ((0, 0), (0, 0), (0, LANES - 2 * heads))).astype(BF16)
    w_if_row = jnp.pad(jnp.swapaxes(w_if, 1, 2), ((0, 0), (0, 16 - 2 * heads), (0, 0))).astype(BF16)
    b_if = b_in[:, off_i:off_qx]
    b_if_col = jnp.pad(b_if, ((0, 0), (0, LANES - 2 * heads)))[:, None, :]
    b_if_row = b_if[:, :, None]
    w_qx = w_in[:, :, off_qx:off_g].astype(BF16)
    b_qx = b_in[:, None, off_qx:off_g]
    w_gate = w_in[:, :, off_g:].astype(BF16)
    b_gate = b_in[:, None, off_g:]
    w_out_b, w_down_b = w_out.astype(BF16), w_down.astype(BF16)
    g_bs_t = jnp.swapaxes(g_bs, 1, 2)
    row3 = lambda a: a[:, None, :]
    g_ln_g3, g_ln_b3 = row3(g_ln_g), row3(g_ln_b)
    conv_b3, norm_g3 = row3(m_conv_b), row3(m_norm_g)
    ln1_g3, ln1_b3, ln2_g3, ln2_b3 = row3(ln1_g), row3(ln1_b), row3(ln2_g), row3(ln2_b)

    mem_n = _layernorm_rows(mem.reshape(batch * mem_len, d), mem_ln_g, mem_ln_b,
                            tm=256, out_dtype=BF16, name="mem_ln")
    xf = x.reshape(n, d)
    xb = xf.astype(BF16)

    tm_big = _tile(seq, 1024)
    identity = functools.partial(_epi_act, lambda a: a)
    for l in range(depth):
        lin = functools.partial(_linear, xb, w_in, b_in3, layer=l, tm=tm_big, tn=1024)
        u_act, = lin(col0=off_u, ncols=g_width, sub=256, row_sub=128,
                     epilogue=functools.partial(_epi_act, _gelu_tanh),
                     outs=[(g_width, F32)], name=f"l{l}_u")
        v_ln, = lin(col0=off_v, ncols=g_width, row_sub=128, epilogue=_epi_gelu_ln,
                    outs=[(g_width, BF16)],
                    extras=[(g_ln_g3, pl.BlockSpec((None, 1, g_width), lambda j, i: (l, 0, 0))),
                            (g_ln_b3, pl.BlockSpec((None, 1, g_width), lambda j, i: (l, 0, 0)))],
                    name=f"l{l}_v")
        qk, = lin(col0=off_qk, ncols=qk_width, sub=256, row_sub=128, epilogue=_epi_conv_silu,
                  prologue=functools.partial(_conv_carry_reset, seq // tm_big),
                  outs=[(qk_width, BF16)],
                  extras=[(m_conv_w, pl.BlockSpec((None, M_CONV, 1024), lambda j, i: (l, 0, j))),
                          (conv_b3, pl.BlockSpec((None, 1, 1024), lambda j, i: (l, 0, j)))],
                  scratch=[pltpu.VMEM((SUBLANES, 1024), F32)], name=f"l{l}_qk")
        v_m, = lin(col0=off_vm, ncols=v_width, epilogue=identity, outs=[(v_width, BF16)],
                   name=f"l{l}_vm")
        o_gate, = lin(col0=off_o, ncols=v_width, sub=256, row_sub=128,
                      epilogue=functools.partial(_epi_act, _sigmoid),
                      outs=[(v_width, F32)], name=f"l{l}_o")
        q_x, = _linear(xb, w_qx, b_qx, l, 0, x_width, tm=tm_big, tn=1024, epilogue=identity,
                       outs=[(x_width, BF16)], name=f"l{l}_qx")
        gcol, grow = _mlstm_gates(xb, w_if_col, w_if_row, b_if_col, b_if_row, l,
                                  tm=tm_big, chunk=chunk, name=f"l{l}_if")

        y_a = _spatial_gating(u_act, v_ln, g_ws, g_bs_t, l, tm=512, name=f"l{l}_sgu")
        y_b = _mlstm_scan(qk, v_m, o_gate, gcol, grow, norm_g3, l, batch=batch, seq=seq,
                          chunk=chunk, group=2, name=f"l{l}_mlstm")
        kv, = _linear(mem_n, x_w_kv, None, l, 0, 2 * x_width, tm=batch * mem_len, tn=1024,
                      epilogue=identity, outs=[(2 * x_width, BF16)], name=f"l{l}_kv")
        y_c = _memory_attention(q_x, kv, batch=batch, seq=seq, mem_len=mem_len, tq=512,
                                name=f"l{l}_xattn")

        merged = _gated_merge(xb, y_a, y_b, y_c, w_gate, b_gate, w_pa, w_pb, w_pc, l,
                              tm=512, tn=512, name=f"l{l}_merge")
        xf, xb = _proj_residual_ln(merged, w_out_b, xf, ln1_g3, ln1_b3, l, alpha, tm=512,
                                   row_sub=128, name=f"l{l}_out_ln1")
        hidden = _ffn_up(xb, w_gu, l, d_ff, tm=tm_big, tn=512, sub=256, name=f"l{l}_ffn_up")
        xf, xb = _proj_residual_ln(hidden, w_down_b, xf, ln2_g3, ln2_b3, l, alpha, tm=256,
                                   row_sub=128, name=f"l{l}_ffn_down_ln2")
    return xf.reshape(batch, seq, d)
```

```python
import functools
import math

import jax
import jax.numpy as jnp
from jax import lax
from jax.experimental import pallas as pl
from jax.experimental.pallas import tpu as pltpu

F32 = jnp.float32
BF16 = jnp.bfloat16

LN_EPS = 1e-5
LANES = 128
SUBLANES = 8
VMEM_LIMIT_BYTES = 48 * 1024 * 1024

G_CHUNK = 128
G_GROUPS = 8
M_HEADS = 4
M_CONV = 4
X_HEADS = 4
N_BRANCH = 3
TAIL_ALIGN = 1024

LAYER_TILING = (
    dict(lin_row_sub=None, lin_sub=None, mlstm_chunk=256, mlstm_group=2, merge_tm=512,
         merge_tn=256, out_tm=512, out_row_sub=128, up_tm=1024, up_sub=256, down_row_sub=128),
    dict(lin_row_sub=None, lin_sub=256, mlstm_chunk=256, mlstm_group=4, merge_tm=512,
         merge_tn=256, out_tm=512, out_row_sub=256, up_tm=1024, up_sub=256, down_row_sub=256),
    dict(lin_row_sub=256, lin_sub=256, mlstm_chunk=256, mlstm_group=1, merge_tm=256,
         merge_tn=256, out_tm=512, out_row_sub=512, up_tm=1024, up_sub=512, down_row_sub=128),
    dict(lin_row_sub=512, lin_sub=256, mlstm_chunk=128, mlstm_group=2, merge_tm=512,
         merge_tn=256, out_tm=256, out_row_sub=256, up_tm=512, up_sub=256, down_row_sub=256),
)


def _params(*sem):
    return pltpu.CompilerParams(dimension_semantics=sem, vmem_limit_bytes=VMEM_LIMIT_BYTES)


def _tile(n, pref):
    t = min(n, pref)
    assert n % t == 0, (n, pref)
    return t


def _sigmoid(x):
    return 1.0 / (1.0 + jnp.exp(-x))


def _silu(x):
    return x * _sigmoid(x)


def _gelu_tanh(x):
    c = math.sqrt(2.0 / math.pi)
    return x * (0.5 * (1.0 + jnp.tanh(c * (x + 0.044715 * (x * x * x)))))


def _log_sigmoid(x):
    return jnp.minimum(x, 0.0) - jnp.log1p(jnp.exp(-jnp.abs(x)))


def _normalize(x):
    mu = jnp.mean(x, axis=-1, keepdims=True)
    xc = x - mu
    var = jnp.mean(xc * xc, axis=-1, keepdims=True)
    return xc * lax.rsqrt(var + LN_EPS)


def _dot(a, b):
    return jnp.dot(a, b, preferred_element_type=F32)


def _dot_nt(a, b):
    return lax.dot_general(a, b, (((1,), (1,)), ((), ())), preferred_element_type=F32)


def _dot_tn(a, b):
    return lax.dot_general(a, b, (((0,), (0,)), ((), ())), preferred_element_type=F32)


def _cast_weight_once(w_ref, wb_ref, row_axis, transposed=False):
    @pl.when(pl.program_id(row_axis) == 0)
    def _():
        if transposed:
            step = min(w_ref.shape[0], 256)
            for r0 in range(0, w_ref.shape[0], step):
                wb_ref[:, r0:r0 + step] = w_ref[r0:r0 + step, :].T.astype(BF16)
        else:
            wb_ref[...] = w_ref[...].astype(BF16)


def _linear(x, w, bias, layer, col0, ncols, *, tm, tn, epilogue, outs, extras=(), scratch=(),
            sub=None, row_sub=None, prologue=None, w_transposed=False, name):
    n, k = x.shape
    tm = _tile(n, tm)
    sub = tn if sub is None else sub
    row_sub = tm if row_sub is None else min(row_sub, tm)
    assert ncols % tn == 0 and col0 % tn == 0 and tn % sub == 0 and tm % row_sub == 0
    j0 = col0 // tn
    grid = (ncols // tn, n // tm)
    has_bias = bias is not None
    cast_w = w.dtype != BF16
    assert cast_w or not w_transposed
    if w_transposed:
        w_spec = pl.BlockSpec((None, tn, k), lambda j, i: (layer, j0 + j, 0))
    else:
        w_spec = pl.BlockSpec((None, k, tn), lambda j, i: (layer, 0, j0 + j))
    in_specs = [pl.BlockSpec((tm, k), lambda j, i: (i, 0)), w_spec]
    args = [x, w]
    if has_bias:
        in_specs.append(pl.BlockSpec((None, 1, tn), lambda j, i: (layer, 0, j0 + j)))
        args.append(bias)
    for arr, spec in extras:
        in_specs.append(spec)
        args.append(arr)
    out_shape = [jax.ShapeDtypeStruct((n, tot), dt) for tot, dt in outs]
    out_specs = [pl.BlockSpec((tm, tn), lambda j, i: (i, j)) for _ in outs]
    n_extra, n_out = len(extras), len(outs)
    scratch = list(scratch) + ([pltpu.VMEM((k, tn), BF16)] if cast_w else [])

    def body(*refs):
        x_ref, w_ref = refs[0], refs[1]
        pos = 2
        b_ref = None
        if has_bias:
            b_ref = refs[pos]
            pos += 1
        extra_refs = refs[pos:pos + n_extra]
        out_refs = refs[pos + n_extra:pos + n_extra + n_out]
        scratch_refs = refs[pos + n_extra + n_out:]
        wb_ref = w_ref
        if cast_w:
            wb_ref, scratch_refs = scratch_refs[-1], scratch_refs[:-1]
            _cast_weight_once(w_ref, wb_ref, 1, w_transposed)
        if prologue is not None:
            prologue(scratch_refs)
        for rb in range(tm // row_sub):
            rows = slice(rb * row_sub, (rb + 1) * row_sub)
            xv = x_ref[rows, :]
            for sb in range(tn // sub):
                cols = slice(sb * sub, (sb + 1) * sub)
                acc = _dot(xv, wb_ref[:, cols])
                if has_bias:
                    acc = acc + b_ref[:, cols]
                epilogue(acc, rows, cols, extra_refs, out_refs, scratch_refs)

    res = pl.pallas_call(
        body, out_shape=out_shape, grid=grid, in_specs=in_specs, out_specs=out_specs,
        scratch_shapes=scratch, compiler_params=_params("parallel", "arbitrary"), name=name,
    )(*args)
    return res


def _epi_act(act, acc, rows, cols, extra_refs, out_refs, scratch_refs):
    out_refs[0][rows, cols] = act(acc).astype(out_refs[0].dtype)


def _epi_gelu_ln(acc, rows, cols, extra_refs, out_refs, scratch_refs):
    g_ref, b_ref = extra_refs
    y = _normalize(_gelu_tanh(acc)) * g_ref[...] + b_ref[...]
    out_refs[0][rows, :] = y.astype(out_refs[0].dtype)


def _conv_carry_reset(tiles_per_seq, scratch_refs):
    carry_ref, = scratch_refs

    @pl.when(pl.program_id(1) % tiles_per_seq == 0)
    def _():
        carry_ref[...] = jnp.zeros_like(carry_ref)


def _epi_conv_silu(acc, rows, cols, extra_refs, out_refs, scratch_refs):
    cw_ref, cb_ref = extra_refs
    carry_ref, = scratch_refs
    tm = acc.shape[0]
    carry = carry_ref[:, cols]
    cw = cw_ref[:, cols]
    y = acc * cw[M_CONV - 1:M_CONV, :] + cb_ref[:, cols]
    top_rows = lax.broadcasted_iota(jnp.int32, (SUBLANES, acc.shape[1]), 0)
    for s in range(1, M_CONV):
        shifted = pltpu.roll(acc, s, axis=0)
        top = jnp.where(top_rows < s, pltpu.roll(carry, s, axis=0), shifted[:SUBLANES])
        shifted = jnp.concatenate([top, shifted[SUBLANES:]], axis=0)
        y = y + shifted * cw[M_CONV - 1 - s:M_CONV - s, :]
    carry_ref[:, cols] = acc[tm - SUBLANES:, :]
    out_refs[0][rows, cols] = _silu(y).astype(out_refs[0].dtype)


def _layernorm_rows(x, g, b, *, tm, out_dtype, name):
    n, d = x.shape
    tm = _tile(n, tm)

    def body(x_ref, g_ref, b_ref, o_ref):
        o_ref[...] = (_normalize(x_ref[...]) * g_ref[...] + b_ref[...]).astype(o_ref.dtype)

    return pl.pallas_call(
        body, out_shape=jax.ShapeDtypeStruct((n, d), out_dtype), grid=(n // tm,),
        in_specs=[pl.BlockSpec((tm, d), lambda i: (i, 0)),
                  pl.BlockSpec((1, d), lambda i: (0, 0)),
                  pl.BlockSpec((1, d), lambda i: (0, 0))],
        out_specs=pl.BlockSpec((tm, d), lambda i: (i, 0)),
        compiler_params=_params("parallel"), name=name,
    )(x, g.reshape(1, d), b.reshape(1, d))


def _split3_bf16(x):
    h1 = x.astype(BF16)
    r1 = x - h1.astype(F32)
    h2 = r1.astype(BF16)
    r2 = r1 - h2.astype(F32)
    return h1, h2, r2.astype(BF16)


def _mlstm_gates(x, w, bias, layer, *, tm, chunk, name):
    n, k = x.shape
    tm = _tile(n, tm)
    assert tm % chunk == 0
    heads = M_HEADS
    rep_w = 2 * heads * LANES

    def body(x_ref, w_ref, b_ref, gcol_ref, grow_ref, wb_ref):
        _cast_weight_once(w_ref, wb_ref, 0, transposed=True)
        z = _dot(x_ref[...], wb_ref[...]) + b_ref[...]
        lf = _log_sigmoid(z)
        r = lax.broadcasted_iota(jnp.int32, (chunk, chunk), 0)
        c = lax.broadcasted_iota(jnp.int32, (chunk, chunk), 1)
        tril = jnp.where(r >= c, 1.0, 0.0).astype(BF16)
        lane = lax.broadcasted_iota(jnp.int32, (chunk, LANES), 1)
        is_f_lane = (lane >= heads) & (lane < 2 * heads)
        for ci in range(tm // chunk):
            sl = slice(ci * chunk, (ci + 1) * chunk)
            a1, a2, a3 = _split3_bf16(lf[sl])
            cs = _dot(tril, a1) + _dot(tril, a2) + _dot(tril, a3)
            gates = jnp.where(is_f_lane, cs, z[sl])
            for c in range(2 * heads):
                gcol_ref[sl, c * LANES:(c + 1) * LANES] = jnp.broadcast_to(
                    gates[:, c:c + 1], (chunk, LANES))
            grow_ref[:, sl] = gates.T[:2 * heads, :]

    return pl.pallas_call(
        body,
        out_shape=[jax.ShapeDtypeStruct((n, rep_w), F32), jax.ShapeDtypeStruct((2 * heads, n), F32)],
        grid=(n // tm,),
        in_specs=[pl.BlockSpec((tm, k), lambda i: (i, 0)),
                  pl.BlockSpec((None, LANES, k), lambda i: (layer, 0, 0)),
                  pl.BlockSpec((None, 1, LANES), lambda i: (layer, 0, 0))],
        out_specs=[pl.BlockSpec((tm, rep_w), lambda i: (i, 0)),
                   pl.BlockSpec((2 * heads, tm), lambda i: (0, i))],
        scratch_shapes=[pltpu.VMEM((k, LANES), BF16)],
        compiler_params=_params("arbitrary"), name=name,
    )(x, w, bias)


def _spatial_gating(u, v, w_s, b_s_t, layer, *, tm, name):
    n, width = u.shape
    tm = _tile(n, tm)
    gd = width // G_GROUPS
    assert tm % G_CHUNK == 0 and gd % LANES == 0

    def body(u_ref, v_ref, w_ref, b_ref, y_ref):
        r = lax.broadcasted_iota(jnp.int32, (G_CHUNK, G_CHUNK), 0)
        c = lax.broadcasted_iota(jnp.int32, (G_CHUNK, G_CHUNK), 1)
        causal = r >= c
        b_t = b_ref[...]
        for g in range(G_GROUPS):
            w = jnp.where(causal, w_ref[g], 0.0).astype(BF16)
            bias = b_t[:, g:g + 1]
            cols = slice(g * gd, (g + 1) * gd)
            for ci in range(tm // G_CHUNK):
                rows = slice(ci * G_CHUNK, (ci + 1) * G_CHUNK)
                mixed = _dot(w, v_ref[rows, cols]) + bias
                y_ref[rows, cols] = (u_ref[rows, cols] * mixed).astype(y_ref.dtype)

    return pl.pallas_call(
        body, out_shape=jax.ShapeDtypeStruct((n, width), BF16), grid=(n // tm,),
        in_specs=[pl.BlockSpec((tm, width), lambda i: (i, 0)),
                  pl.BlockSpec((tm, width), lambda i: (i, 0)),
                  pl.BlockSpec((None, G_GROUPS, G_CHUNK, G_CHUNK), lambda i: (layer, 0, 0, 0)),
                  pl.BlockSpec((None, G_CHUNK, G_GROUPS), lambda i: (layer, 0, 0))],
        out_specs=pl.BlockSpec((tm, width), lambda i: (i, 0)),
        compiler_params=_params("parallel"), name=name,
    )(u, v, w_s, b_s_t)


def _lane_blocks(a):
    return [a[:, j * LANES:(j + 1) * LANES] for j in range(a.shape[1] // LANES)]


def _scale_rows(a, r):
    return jnp.concatenate([blk * r for blk in _lane_blocks(a)], axis=1)


def _mlstm_scan(qk, v, o_gate, gcol, grow, norm_g, layer, *, batch, seq, chunk, group, name):
    n = batch * seq
    heads = M_HEADS
    dk = qk.shape[1] // (2 * heads)
    dv = v.shape[1] // heads
    nc = seq // chunk
    assert seq % chunk == 0
    k_scale = dk ** -0.5

    def body(q_ref, k_ref, v_ref, o_ref, gcol_ref, grow_ref, ng_ref, y_ref, state_ref, m_ref):
        @pl.when(pl.program_id(1) == 0)
        def _():
            state_ref[...] = jnp.zeros_like(state_ref)
            m_ref[...] = jnp.zeros_like(m_ref)

        grow_v = grow_ref[...]
        r = lax.broadcasted_iota(jnp.int32, (chunk, LANES), 0)
        c = lax.broadcasted_iota(jnp.int32, (chunk, LANES), 1)
        causal = [r >= c + j * LANES for j in range(chunk // LANES)]
        ones = jnp.ones((chunk, LANES), BF16)

        def head_group(hs):
            i_col = {h: gcol_ref[:, h * LANES:(h + 1) * LANES] for h in hs}
            b_col = {h: gcol_ref[:, (heads + h) * LANES:(heads + h + 1) * LANES] for h in hs}
            src_row = {h: _lane_blocks(grow_v[h:h + 1, :] - grow_v[heads + h:heads + h + 1, :])
                       for h in hs}
            m_prev = {h: m_ref[h][0:1, :] for h in hs}
            q = {h: q_ref[:, h * dk:(h + 1) * dk] for h in hs}
            k = {h: k_ref[:, h * dk:(h + 1) * dk] for h in hs}
            vh = {h: v_ref[:, h * dv:(h + 1) * dv] for h in hs}

            scores = {h: _lane_blocks(_dot_nt(q[h], k[h])) for h in hs}
            q_state = {h: _dot(q[h], state_ref[h].astype(BF16)) for h in hs}
            d = {h: [jnp.where(causal[j], b_col[h] + src_row[h][j], -jnp.inf)
                     for j in range(len(causal))] for h in hs}
            m_inter = {h: b_col[h] + m_prev[h] for h in hs}
            m_row = {h: jnp.maximum(
                jnp.max(functools.reduce(jnp.maximum, d[h]), axis=-1, keepdims=True), m_inter[h])
                for h in hs}
            p = {h: jnp.concatenate(
                [jnp.exp(d[h][j] - m_row[h]) * (scores[h][j] * k_scale) for j in range(len(causal))],
                axis=1) for h in hs}
            w_inter = {h: jnp.exp(m_inter[h] - m_row[h]) for h in hs}
            pv = {h: _dot(p[h].astype(BF16), vh[h]) for h in hs}
            den = {h: jnp.sum(p[h], axis=-1, keepdims=True) + w_inter[h] * q_state[h][:, dv:]
                   for h in hs}
            inv = {h: 1.0 / jnp.maximum(jnp.abs(den[h]), jnp.exp(-m_row[h])) for h in hs}
            hn = {h: _normalize(_scale_rows(pv[h] + _scale_rows(q_state[h][:, :dv], w_inter[h]),
                                            inv[h])) for h in hs}
            for h in hs:
                cols = slice(h * dv, (h + 1) * dv)
                y_ref[:, cols] = (o_ref[:, cols] * (hn[h] * ng_ref[:, cols])).astype(y_ref.dtype)

            b_last = {h: b_col[h][chunk - 1:chunk, :] for h in hs}
            g = {h: b_last[h] - b_col[h] + i_col[h] for h in hs}
            m_new = {h: jnp.maximum(b_last[h] + m_prev[h], jnp.max(g[h], axis=0, keepdims=True))
                     for h in hs}
            w_state = {h: jnp.exp(g[h] - m_new[h]) * k_scale for h in hs}
            decay = {h: jnp.exp(b_last[h] + m_prev[h] - m_new[h]) for h in hs}
            kw = {h: _scale_rows(k[h].astype(F32), w_state[h]).astype(BF16) for h in hs}
            upd = {h: _dot_tn(kw[h], jnp.concatenate([vh[h], ones], axis=1)) for h in hs}
            for h in hs:
                state_ref[h] = decay[h][:, :1] * state_ref[h] + upd[h]
                m_ref[h] = jnp.broadcast_to(m_new[h], (SUBLANES, LANES))

        for h0 in range(0, heads, group):
            head_group(range(h0, min(h0 + group, heads)))

    return pl.pallas_call(
        body, out_shape=jax.ShapeDtypeStruct((n, heads * dv), BF16), grid=(batch, nc),
        in_specs=[pl.BlockSpec((chunk, heads * dk), lambda b, c: (b * nc + c, 0)),
                  pl.BlockSpec((chunk, heads * dk), lambda b, c: (b * nc + c, 1)),
                  pl.BlockSpec((chunk, heads * dv), lambda b, c: (b * nc + c, 0)),
                  pl.BlockSpec((chunk, heads * dv), lambda b, c: (b * nc + c, 0)),
                  pl.BlockSpec((chunk, 2 * heads * LANES), lambda b, c: (b * nc + c, 0)),
                  pl.BlockSpec((2 * heads, chunk), lambda b, c: (0, b * nc + c)),
                  pl.BlockSpec((None, 1, heads * dv), lambda b, c: (layer, 0, 0))],
        out_specs=pl.BlockSpec((chunk, heads * dv), lambda b, c: (b * nc + c, 0)),
        scratch_shapes=[pltpu.VMEM((heads, dk, dv + LANES), F32),
                        pltpu.VMEM((heads, SUBLANES, LANES), F32)],
        compiler_params=_params("parallel", "arbitrary"), name=name,
    )(qk, qk, v, o_gate, gcol, grow, norm_g)


def _memory_attention(q, kv, *, batch, seq, mem_len, tq, name):
    n, width = q.shape
    hd = width // X_HEADS
    tq = _tile(seq, tq)
    nq = seq // tq
    scale = hd ** -0.5

    def body(q_ref, kv_ref, y_ref):
        for h in range(X_HEADS):
            cols = slice(h * hd, (h + 1) * hd)
            s = _dot_nt(q_ref[:, cols], kv_ref[:, cols]) * scale
            e = jnp.exp(s - jnp.max(s, axis=-1, keepdims=True))
            p = e * (1.0 / jnp.sum(e, axis=-1, keepdims=True))
            y_ref[:, cols] = _dot(p.astype(BF16), kv_ref[:, width + h * hd:width + (h + 1) * hd]
                                  ).astype(y_ref.dtype)

    return pl.pallas_call(
        body, out_shape=jax.ShapeDtypeStruct((n, width), BF16), grid=(batch, nq),
        in_specs=[pl.BlockSpec((tq, width), lambda b, i: (b * nq + i, 0)),
                  pl.BlockSpec((mem_len, 2 * width), lambda b, i: (b, 0))],
        out_specs=pl.BlockSpec((tq, width), lambda b, i: (b * nq + i, 0)),
        compiler_params=_params("parallel", "parallel"), name=name,
    )(q, kv)


def _gated_merge(x, y_a, y_b, y_c, w_g, b_g, gate_col0, w_pa, w_pb, w_pc, layer, *, tm, tn, name):
    n, d = x.shape
    tm = _tile(n, tm)
    nj = d // tn
    assert gate_col0 % tn == 0
    jg0 = gate_col0 // tn

    def body(x_ref, ya_ref, yb_ref, yc_ref, wg0, wg1, wg2, bg0, bg1, bg2, wa, wb, wc, o_ref,
             wa_b, wb_b, wc_b, wg0_b, wg1_b, wg2_b):
        for w_ref, wb_ref in ((wa, wa_b), (wb, wb_b), (wc, wc_b)):
            _cast_weight_once(w_ref, wb_ref, 1)
        for w_ref, wb_ref in ((wg0, wg0_b), (wg1, wg1_b), (wg2, wg2_b)):
            _cast_weight_once(w_ref, wb_ref, 1, transposed=True)
        xv = x_ref[...]
        acc = _sigmoid(_dot(xv, wg0_b[...]) + bg0[...]) * _dot(ya_ref[...], wa_b[...])
        acc = acc + _sigmoid(_dot(xv, wg1_b[...]) + bg1[...]) * _dot(yb_ref[...], wb_b[...])
        acc = acc + _sigmoid(_dot(xv, wg2_b[...]) + bg2[...]) * _dot(yc_ref[...], wc_b[...])
        o_ref[...] = acc.astype(o_ref.dtype)

    def act_spec(arr):
        return pl.BlockSpec((tm, arr.shape[1]), lambda j, i: (i, 0))

    def gate_w_spec(k):
        return pl.BlockSpec((None, tn, d), lambda j, i: (layer, jg0 + k * nj + j, 0))

    def gate_b_spec(k):
        return pl.BlockSpec((None, 1, tn), lambda j, i: (layer, 0, jg0 + k * nj + j))

    def proj_spec(w):
        return pl.BlockSpec((None, w.shape[1], tn), lambda j, i: (layer, 0, j))

    return pl.pallas_call(
        body, out_shape=jax.ShapeDtypeStruct((n, d), BF16), grid=(nj, n // tm),
        in_specs=[act_spec(x), act_spec(y_a), act_spec(y_b), act_spec(y_c),
                  gate_w_spec(0), gate_w_spec(1), gate_w_spec(2),
                  gate_b_spec(0), gate_b_spec(1), gate_b_spec(2),
                  proj_spec(w_pa), proj_spec(w_pb), proj_spec(w_pc)],
        out_specs=pl.BlockSpec((tm, tn), lambda j, i: (i, j)),
        scratch_shapes=([pltpu.VMEM((w.shape[1], tn), BF16) for w in (w_pa, w_pb, w_pc)]
                        + [pltpu.VMEM((d, tn), BF16)] * N_BRANCH),
        compiler_params=_params("parallel", "arbitrary"), name=name,
    )(x, y_a, y_b, y_c, w_g, w_g, w_g, b_g, b_g, b_g, w_pa, w_pb, w_pc)


def _proj_residual_ln(a, w, res, ln_g, ln_b, layer, alpha, *, tm, row_sub, name):
    n, k = a.shape
    d = w.shape[2]
    tm = _tile(n, tm)
    row_sub = min(row_sub, tm)
    assert tm % row_sub == 0

    def body(a_ref, w_ref, r_ref, g_ref, b_ref, of_ref, ob_ref):
        for rb in range(tm // row_sub):
            rows = slice(rb * row_sub, (rb + 1) * row_sub)
            pre = alpha * r_ref[rows, :] + _dot(a_ref[rows, :], w_ref[...])
            y = _normalize(pre) * g_ref[...] + b_ref[...]
            of_ref[rows, :] = y
            ob_ref[rows, :] = y.astype(BF16)

    return pl.pallas_call(
        body,
        out_shape=[jax.ShapeDtypeStruct((n, d), F32), jax.ShapeDtypeStruct((n, d), BF16)],
        grid=(n // tm,),
        in_specs=[pl.BlockSpec((tm, k), lambda i: (i, 0)),
                  pl.BlockSpec((None, k, d), lambda i: (layer, 0, 0), pipeline_mode=pl.Buffered(1)),
                  pl.BlockSpec((tm, d), lambda i: (i, 0)),
                  pl.BlockSpec((None, 1, d), lambda i: (layer, 0, 0)),
                  pl.BlockSpec((None, 1, d), lambda i: (layer, 0, 0))],
        out_specs=[pl.BlockSpec((tm, d), lambda i: (i, 0)), pl.BlockSpec((tm, d), lambda i: (i, 0))],
        compiler_params=_params("parallel"), name=name,
    )(a, w, res, ln_g, ln_b)


def _ffn_up(x, w_gu, layer, d_ff, *, tm, tn, sub, name):
    n, d = x.shape
    tm = _tile(n, tm)
    nj = d_ff // tn
    assert d_ff % tn == 0 and tn % sub == 0

    def body(x_ref, wg_ref, wu_ref, o_ref, wg_b, wu_b):
        _cast_weight_once(wg_ref, wg_b, 1)
        _cast_weight_once(wu_ref, wu_b, 1)
        xv = x_ref[...]
        for sb in range(tn // sub):
            cols = slice(sb * sub, (sb + 1) * sub)
            o_ref[:, cols] = (_silu(_dot(xv, wg_b[:, cols])) * _dot(xv, wu_b[:, cols])
                              ).astype(o_ref.dtype)

    return pl.pallas_call(
        body, out_shape=jax.ShapeDtypeStruct((n, d_ff), BF16), grid=(nj, n // tm),
        in_specs=[pl.BlockSpec((tm, d), lambda j, i: (i, 0)),
                  pl.BlockSpec((None, d, tn), lambda j, i: (layer, 0, j)),
                  pl.BlockSpec((None, d, tn), lambda j, i: (layer, 0, nj + j))],
        out_specs=pl.BlockSpec((tm, tn), lambda j, i: (i, j)),
        scratch_shapes=[pltpu.VMEM((d, tn), BF16), pltpu.VMEM((d, tn), BF16)],
        compiler_params=_params("parallel", "arbitrary"), name=name,
    )(x, w_gu, w_gu)


def kernel(x, mem, mem_ln_g, mem_ln_b, w_in, b_in, g_ln_g, g_ln_b, g_ws, g_bs, m_conv_w, m_conv_b,
           m_norm_g, x_w_kv, w_pa, w_pb, w_pc, w_out, ln1_g, ln1_b, w_gu, w_down, ln2_g, ln2_b):
    batch, seq, d = x.shape
    depth = w_in.shape[0]
    mem_len = mem.shape[1]
    n = batch * seq
    g_width = g_ws.shape[1] * (d // 16)
    qk_width = m_conv_w.shape[2]
    v_width = m_norm_g.shape[1]
    x_width = x_w_kv.shape[2] // 2
    d_ff = w_down.shape[1]
    alpha = (2 * depth) ** 0.25
    heads = M_HEADS

    off_u, off_v = 0, g_width
    off_qk = 2 * g_width
    off_vm = off_qk + qk_width
    off_o = off_vm + v_width
    off_i = off_o + v_width
    off_qx = off_i + 2 * heads
    off_g = off_qx + x_width
    assert w_in.shape[2] == off_g + N_BRANCH * d

    b_in3 = b_in[:, None, :]
    w_in_t = jnp.swapaxes(w_in, 1, 2)
    tail_pad = TAIL_ALIGN - 2 * heads
    w_tail = jnp.concatenate(
        [w_in_t[:, off_i:off_qx, :], jnp.zeros((depth, tail_pad, d), F32), w_in_t[:, off_qx:, :]],
        axis=1)
    b_tail = jnp.concatenate(
        [b_in[:, off_i:off_qx], jnp.zeros((depth, tail_pad), F32), b_in[:, off_qx:]],
        axis=1)[:, None, :]
    tail_qx, tail_g = TAIL_ALIGN, TAIL_ALIGN + x_width
    w_out_b, w_down_b = w_out.astype(BF16), w_down.astype(BF16)
    g_bs_t = jnp.swapaxes(g_bs, 1, 2)
    row3 = lambda a: a[:, None, :]
    g_ln_g3, g_ln_b3 = row3(g_ln_g), row3(g_ln_b)
    conv_b3, norm_g3 = row3(m_conv_b), row3(m_norm_g)
    ln1_g3, ln1_b3, ln2_g3, ln2_b3 = row3(ln1_g), row3(ln1_b), row3(ln2_g), row3(ln2_b)

    mem_n = _layernorm_rows(mem.reshape(batch * mem_len, d), mem_ln_g, mem_ln_b,
                            tm=256, out_dtype=BF16, name="mem_ln")
    xf = x.reshape(n, d)
    xb = xf.astype(BF16)

    tm_big = _tile(seq, 1024)
    identity = functools.partial(_epi_act, lambda a: a)
    for l in range(depth):
        cfg = LAYER_TILING[l % len(LAYER_TILING)]
        rs, cs = cfg["lin_row_sub"], cfg["lin_sub"]
        chunk = min(cfg["mlstm_chunk"], seq)
        lin = functools.partial(_linear, xb, w_in_t, b_in3, layer=l, tm=tm_big, tn=1024,
                                w_transposed=True)
        u_act, = lin(col0=off_u, ncols=g_width, sub=cs, row_sub=rs,
                     epilogue=functools.partial(_epi_act, _gelu_tanh),
                     outs=[(g_width, F32)], name=f"l{l}_u")
        v_ln, = lin(col0=off_v, ncols=g_width, row_sub=rs, epilogue=_epi_gelu_ln,
                    outs=[(g_width, BF16)],
                    extras=[(g_ln_g3, pl.BlockSpec((None, 1, g_width), lambda j, i: (l, 0, 0))),
                            (g_ln_b3, pl.BlockSpec((None, 1, g_width), lambda j, i: (l, 0, 0)))],
                    name=f"l{l}_v")
        qk, = lin(col0=off_qk, ncols=qk_width, sub=cs, row_sub=rs, epilogue=_epi_conv_silu,
                  prologue=functools.partial(_conv_carry_reset, seq // tm_big),
                  outs=[(qk_width, BF16)],
                  extras=[(m_conv_w, pl.BlockSpec((None, M_CONV, 1024), lambda j, i: (l, 0, j))),
                          (conv_b3, pl.BlockSpec((None, 1, 1024), lambda j, i: (l, 0, j)))],
                  scratch=[pltpu.VMEM((SUBLANES, 1024), F32)], name=f"l{l}_qk")
        v_m, = lin(col0=off_vm, ncols=v_width, epilogue=identity, outs=[(v_width, BF16)],
                   name=f"l{l}_vm")
        o_gate, = lin(col0=off_o, ncols=v_width, sub=cs, row_sub=rs,
                      epilogue=functools.partial(_epi_act, _sigmoid),
                      outs=[(v_width, F32)], name=f"l{l}_o")
        q_x, = _linear(xb, w_tail, b_tail, l, tail_qx, x_width, tm=tm_big, tn=1024,
                       w_transposed=True, epilogue=identity, outs=[(x_width, BF16)],
                       name=f"l{l}_qx")
        gcol, grow = _mlstm_gates(xb, w_tail, b_tail, l, tm=tm_big, chunk=chunk, name=f"l{l}_if")

        y_a = _spatial_gating(u_act, v_ln, g_ws, g_bs_t, l, tm=512, name=f"l{l}_sgu")
        y_b = _mlstm_scan(qk, v_m, o_gate, gcol, grow, norm_g3, l, batch=batch, seq=seq,
                          chunk=chunk, group=cfg["mlstm_group"], name=f"l{l}_mlstm")
        kv, = _linear(mem_n, x_w_kv, None, l, 0, 2 * x_width, tm=batch * mem_len, tn=1024,
                      epilogue=identity, outs=[(2 * x_width, BF16)], name=f"l{l}_kv")
        y_c = _memory_attention(q_x, kv, batch=batch, seq=seq, mem_len=mem_len, tq=512,
                                name=f"l{l}_xattn")

        merged = _gated_merge(xb, y_a, y_b, y_c, w_tail, b_tail, tail_g, w_pa, w_pb, w_pc, l,
                              tm=cfg["merge_tm"], tn=cfg["merge_tn"], name=f"l{l}_merge")
        xf, xb = _proj_residual_ln(merged, w_out_b, xf, ln1_g3, ln1_b3, l, alpha,
                                   tm=cfg["out_tm"], row_sub=cfg["out_row_sub"],
                                   name=f"l{l}_out_ln1")
        hidden = _ffn_up(xb, w_gu, l, d_ff, tm=cfg["up_tm"], tn=512, sub=cfg["up_sub"],
                         name=f"l{l}_ffn_up")
        xf, xb = _proj_residual_ln(hidden, w_down_b, xf, ln2_g3, ln2_b3, l, alpha, tm=256,
                                   row_sub=cfg["down_row_sub"], name=f"l{l}_ffn_down_ln2")
    return xf.reshape(batch, seq, d)
```

```python
import functools
import math

import jax
import jax.numpy as jnp
from jax import lax
from jax.experimental import pallas as pl
from jax.experimental.pallas import tpu as pltpu

F32 = jnp.float32
BF16 = jnp.bfloat16

LN_EPS = 1e-5
LANES = 128
SUBLANES = 8
VMEM_LIMIT_BYTES = 48 * 1024 * 1024

G_CHUNK = 128
G_GROUPS = 8
M_HEADS = 4
M_CONV = 4
X_HEADS = 4
N_BRANCH = 3
M_SCAN_CHUNK = 256

LAYER_TILING = (
    dict(merge_tm=512, merge_tn=512, merge_wbuf=1),
    dict(merge_tm=256, merge_tn=512, merge_wbuf=1),
    dict(merge_tm=512, merge_tn=256, merge_wbuf=2),
    dict(merge_tm=512, merge_tn=512, merge_wbuf=1),
)


def _params(*sem):
    return pltpu.CompilerParams(dimension_semantics=sem, vmem_limit_bytes=VMEM_LIMIT_BYTES)


def _tile(n, pref):
    t = min(n, pref)
    assert n % t == 0, (n, pref)
    return t


def _sigmoid(x):
    return 1.0 / (1.0 + jnp.exp(-x))


def _silu(x):
    return x * _sigmoid(x)


def _gelu_tanh(x):
    c = math.sqrt(2.0 / math.pi)
    return x * (0.5 * (1.0 + jnp.tanh(c * (x + 0.044715 * (x * x * x)))))


def _log_sigmoid(x):
    return jnp.minimum(x, 0.0) - jnp.log1p(jnp.exp(-jnp.abs(x)))


def _normalize(x):
    mu = jnp.mean(x, axis=-1, keepdims=True)
    xc = x - mu
    var = jnp.mean(xc * xc, axis=-1, keepdims=True)
    return xc * lax.rsqrt(var + LN_EPS)


def _dot(a, b):
    return jnp.dot(a, b, preferred_element_type=F32)


def _dot_nt(a, b):
    return lax.dot_general(a, b, (((1,), (1,)), ((), ())), preferred_element_type=F32)


def _dot_tn(a, b):
    return lax.dot_general(a, b, (((0,), (0,)), ((), ())), preferred_element_type=F32)


def _cast_weight_once(w_ref, wb_ref, row_axis, transposed=False):
    @pl.when(pl.program_id(row_axis) == 0)
    def _():
        if transposed:
            w2d = w_ref.at[0] if len(w_ref.shape) == 3 else w_ref
            step = min(w2d.shape[0], 256)
            for r0 in range(0, w2d.shape[0], step):
                wb_ref[:, r0:r0 + step] = w2d[r0:r0 + step, :].T.astype(BF16)
        else:
            wb_ref[...] = w_ref[...].astype(BF16)


def _weight_rows_spec(rows, k, index_map, mode):
    return pl.BlockSpec((pl.Element(1), pl.Element(rows), pl.Element(k)), index_map,
                        pipeline_mode=mode)


def _linear(x, w, bias, layer, col0, ncols, *, tm, tn, epilogue, outs, extras=(), scratch=(),
            sub=None, row_sub=None, prologue=None, w_transposed=False, w_row0=None, name):
    n, k = x.shape
    tm = _tile(n, tm)
    sub = tn if sub is None else sub
    row_sub = tm if row_sub is None else min(row_sub, tm)
    assert ncols % tn == 0 and col0 % tn == 0 and tn % sub == 0 and tm % row_sub == 0
    j0 = col0 // tn
    grid = (ncols // tn, n // tm)
    has_bias = bias is not None
    cast_w = w.dtype != BF16
    assert cast_w or not w_transposed
    if w_transposed:
        row0 = col0 if w_row0 is None else w_row0
        assert row0 % SUBLANES == 0
        w_spec = _weight_rows_spec(
            tn, k, lambda j, i: (layer, pl.multiple_of(row0 + j * tn, SUBLANES), 0),
            pl.Buffered(2))
    else:
        w_spec = pl.BlockSpec((None, k, tn), lambda j, i: (layer, 0, j0 + j))
    in_specs = [pl.BlockSpec((tm, k), lambda j, i: (i, 0)), w_spec]
    args = [x, w]
    if has_bias:
        in_specs.append(pl.BlockSpec((None, 1, tn), lambda j, i: (layer, 0, j0 + j)))
        args.append(bias)
    for arr, spec in extras:
        in_specs.append(spec)
        args.append(arr)
    out_shape = [jax.ShapeDtypeStruct((n, tot), dt) for tot, dt in outs]
    out_specs = [pl.BlockSpec((tm, tn), lambda j, i: (i, j)) for _ in outs]
    n_extra, n_out = len(extras), len(outs)
    scratch = list(scratch) + ([pltpu.VMEM((k, tn), BF16)] if cast_w else [])

    def body(*refs):
        x_ref, w_ref = refs[0], refs[1]
        pos = 2
        b_ref = None
        if has_bias:
            b_ref = refs[pos]
            pos += 1
        extra_refs = refs[pos:pos + n_extra]
        out_refs = refs[pos + n_extra:pos + n_extra + n_out]
        scratch_refs = refs[pos + n_extra + n_out:]
        wb_ref = w_ref
        if cast_w:
            wb_ref, scratch_refs = scratch_refs[-1], scratch_refs[:-1]
            _cast_weight_once(w_ref, wb_ref, 1, w_transposed)
        if prologue is not None:
            prologue(scratch_refs)
        for rb in range(tm // row_sub):
            rows = slice(rb * row_sub, (rb + 1) * row_sub)
            xv = x_ref[rows, :]
            for sb in range(tn // sub):
                cols = slice(sb * sub, (sb + 1) * sub)
                acc = _dot(xv, wb_ref[:, cols])
                if has_bias:
                    acc = acc + b_ref[:, cols]
                epilogue(acc, rows, cols, extra_refs, out_refs, scratch_refs)

    res = pl.pallas_call(
        body, out_shape=out_shape, grid=grid, in_specs=in_specs, out_specs=out_specs,
        scratch_shapes=scratch, compiler_params=_params("parallel", "arbitrary"), name=name,
    )(*args)
    return res


def _epi_act(act, acc, rows, cols, extra_refs, out_refs, scratch_refs):
    out_refs[0][rows, cols] = act(acc).astype(out_refs[0].dtype)


def _epi_gelu_ln(acc, rows, cols, extra_refs, out_refs, scratch_refs):
    g_ref, b_ref = extra_refs
    y = _normalize(_gelu_tanh(acc)) * g_ref[...] + b_ref[...]
    out_refs[0][rows, :] = y.astype(out_refs[0].dtype)


def _conv_carry_reset(tiles_per_seq, scratch_refs):
    carry_ref, = scratch_refs

    @pl.when(pl.program_id(1) % tiles_per_seq == 0)
    def _():
        carry_ref[...] = jnp.zeros_like(carry_ref)


def _epi_conv_silu(acc, rows, cols, extra_refs, out_refs, scratch_refs):
    cw_ref, cb_ref = extra_refs
    carry_ref, = scratch_refs
    tm = acc.shape[0]
    carry = carry_ref[:, cols]
    cw = cw_ref[:, cols]
    y = acc * cw[M_CONV - 1:M_CONV, :] + cb_ref[:, cols]
    top_rows = lax.broadcasted_iota(jnp.int32, (SUBLANES, acc.shape[1]), 0)
    for s in range(1, M_CONV):
        shifted = pltpu.roll(acc, s, axis=0)
        top = jnp.where(top_rows < s, pltpu.roll(carry, s, axis=0), shifted[:SUBLANES])
        shifted = jnp.concatenate([top, shifted[SUBLANES:]], axis=0)
        y = y + shifted * cw[M_CONV - 1 - s:M_CONV - s, :]
    carry_ref[:, cols] = acc[tm - SUBLANES:, :]
    out_refs[0][rows, cols] = _silu(y).astype(out_refs[0].dtype)


def _layernorm_rows(x, g, b, *, tm, out_dtype, name):
    n, d = x.shape
    tm = _tile(n, tm)

    def body(x_ref, g_ref, b_ref, o_ref):
        o_ref[...] = (_normalize(x_ref[...]) * g_ref[...] + b_ref[...]).astype(o_ref.dtype)

    return pl.pallas_call(
        body, out_shape=jax.ShapeDtypeStruct((n, d), out_dtype), grid=(n // tm,),
        in_specs=[pl.BlockSpec((tm, d), lambda i: (i, 0)),
                  pl.BlockSpec((1, d), lambda i: (0, 0)),
                  pl.BlockSpec((1, d), lambda i: (0, 0))],
        out_specs=pl.BlockSpec((tm, d), lambda i: (i, 0)),
        compiler_params=_params("parallel"), name=name,
    )(x, g.reshape(1, d), b.reshape(1, d))


def _split3_bf16(x):
    h1 = x.astype(BF16)
    r1 = x - h1.astype(F32)
    h2 = r1.astype(BF16)
    r2 = r1 - h2.astype(F32)
    return h1, h2, r2.astype(BF16)


def _mlstm_gates(x, w, bias, layer, col0, *, tm, chunk, name):
    n, k = x.shape
    tm = _tile(n, tm)
    assert tm % chunk == 0 and col0 % LANES == 0
    heads = M_HEADS
    rep_w = 2 * heads * LANES

    def body(x_ref, w_ref, b_ref, gcol_ref, grow_ref, wb_ref):
        _cast_weight_once(w_ref, wb_ref, 0, transposed=True)
        z = _dot(x_ref[...], wb_ref[...]) + b_ref[...]
        lf = _log_sigmoid(z)
        r = lax.broadcasted_iota(jnp.int32, (chunk, chunk), 0)
        c = lax.broadcasted_iota(jnp.int32, (chunk, chunk), 1)
        tril = jnp.where(r >= c, 1.0, 0.0).astype(BF16)
        lane = lax.broadcasted_iota(jnp.int32, (chunk, LANES), 1)
        is_f_lane = (lane >= heads) & (lane < 2 * heads)
        for ci in range(tm // chunk):
            sl = slice(ci * chunk, (ci + 1) * chunk)
            a1, a2, a3 = _split3_bf16(lf[sl])
            cs = _dot(tril, a1) + _dot(tril, a2) + _dot(tril, a3)
            gates = jnp.where(is_f_lane, cs, z[sl])
            for c in range(2 * heads):
                gcol_ref[sl, c * LANES:(c + 1) * LANES] = jnp.broadcast_to(
                    gates[:, c:c + 1], (chunk, LANES))
            grow_ref[:, sl] = gates.T[:2 * heads, :]

    return pl.pallas_call(
        body,
        out_shape=[jax.ShapeDtypeStruct((n, rep_w), F32), jax.ShapeDtypeStruct((2 * heads, n), F32)],
        grid=(n // tm,),
        in_specs=[pl.BlockSpec((tm, k), lambda i: (i, 0)),
                  pl.BlockSpec((None, LANES, k), lambda i: (layer, col0 // LANES, 0)),
                  pl.BlockSpec((None, 1, LANES), lambda i: (layer, 0, col0 // LANES))],
        out_specs=[pl.BlockSpec((tm, rep_w), lambda i: (i, 0)),
                   pl.BlockSpec((2 * heads, tm), lambda i: (0, i))],
        scratch_shapes=[pltpu.VMEM((k, LANES), BF16)],
        compiler_params=_params("arbitrary"), name=name,
    )(x, w, bias)


def _spatial_gating(u, v, w_s, b_s_t, layer, *, tm, name):
    n, width = u.shape
    tm = _tile(n, tm)
    gd = width // G_GROUPS
    assert tm % G_CHUNK == 0 and gd % LANES == 0

    def body(u_ref, v_ref, w_ref, b_ref, y_ref):
        r = lax.broadcasted_iota(jnp.int32, (G_CHUNK, G_CHUNK), 0)
        c = lax.broadcasted_iota(jnp.int32, (G_CHUNK, G_CHUNK), 1)
        causal = r >= c
        b_t = b_ref[...]
        for g in range(G_GROUPS):
            w = jnp.where(causal, w_ref[g], 0.0).astype(BF16)
            bias = b_t[:, g:g + 1]
            cols = slice(g * gd, (g + 1) * gd)
            for ci in range(tm // G_CHUNK):
                rows = slice(ci * G_CHUNK, (ci + 1) * G_CHUNK)
                mixed = _dot(w, v_ref[rows, cols]) + bias
                y_ref[rows, cols] = (u_ref[rows, cols] * mixed).astype(y_ref.dtype)

    return pl.pallas_call(
        body, out_shape=jax.ShapeDtypeStruct((n, width), BF16), grid=(n // tm,),
        in_specs=[pl.BlockSpec((tm, width), lambda i: (i, 0)),
                  pl.BlockSpec((tm, width), lambda i: (i, 0)),
                  pl.BlockSpec((None, G_GROUPS, G_CHUNK, G_CHUNK), lambda i: (layer, 0, 0, 0)),
                  pl.BlockSpec((None, G_CHUNK, G_GROUPS), lambda i: (layer, 0, 0))],
        out_specs=pl.BlockSpec((tm, width), lambda i: (i, 0)),
        compiler_params=_params("parallel"), name=name,
    )(u, v, w_s, b_s_t)


def _lane_blocks(a):
    return [a[:, j * LANES:(j + 1) * LANES] for j in range(a.shape[1] // LANES)]


def _scale_rows(a, r):
    return jnp.concatenate([blk * r for blk in _lane_blocks(a)], axis=1)


def _mlstm_scan(qk, v, o_gate, gcol, grow, norm_g, layer, *, batch, seq, chunk, group, name):
    n = batch * seq
    heads = M_HEADS
    dk = qk.shape[1] // (2 * heads)
    dv = v.shape[1] // heads
    nc = seq // chunk
    assert seq % chunk == 0
    k_scale = dk ** -0.5

    def body(q_ref, k_ref, v_ref, o_ref, gcol_ref, grow_ref, ng_ref, y_ref, state_ref, m_ref):
        @pl.when(pl.program_id(1) == 0)
        def _():
            state_ref[...] = jnp.zeros_like(state_ref)
            m_ref[...] = jnp.zeros_like(m_ref)

        grow_v = grow_ref[...]
        r = lax.broadcasted_iota(jnp.int32, (chunk, LANES), 0)
        c = lax.broadcasted_iota(jnp.int32, (chunk, LANES), 1)
        causal = [r >= c + j * LANES for j in range(chunk // LANES)]
        ones = jnp.ones((chunk, LANES), BF16)

        def head_group(hs):
            i_col = {h: gcol_ref[:, h * LANES:(h + 1) * LANES] for h in hs}
            b_col = {h: gcol_ref[:, (heads + h) * LANES:(heads + h + 1) * LANES] for h in hs}
            src_row = {h: _lane_blocks(grow_v[h:h + 1, :] - grow_v[heads + h:heads + h + 1, :])
                       for h in hs}
            m_prev = {h: m_ref[h][0:1, :] for h in hs}
            q = {h: q_ref[:, h * dk:(h + 1) * dk] for h in hs}
            k = {h: k_ref[:, h * dk:(h + 1) * dk] for h in hs}
            vh = {h: v_ref[:, h * dv:(h + 1) * dv] for h in hs}

            scores = {h: _lane_blocks(_dot_nt(q[h], k[h])) for h in hs}
            q_state = {h: _dot(q[h], state_ref[h].astype(BF16)) for h in hs}
            d = {h: [jnp.where(causal[j], b_col[h] + src_row[h][j], -jnp.inf)
                     for j in range(len(causal))] for h in hs}
            m_inter = {h: b_col[h] + m_prev[h] for h in hs}
            m_row = {h: jnp.maximum(
                jnp.max(functools.reduce(jnp.maximum, d[h]), axis=-1, keepdims=True), m_inter[h])
                for h in hs}
            p = {h: jnp.concatenate(
                [jnp.exp(d[h][j] - m_row[h]) * (scores[h][j] * k_scale) for j in range(len(causal))],
                axis=1) for h in hs}
            w_inter = {h: jnp.exp(m_inter[h] - m_row[h]) for h in hs}
            pv = {h: _dot(p[h].astype(BF16), vh[h]) for h in hs}
            den = {h: jnp.sum(p[h], axis=-1, keepdims=True) + w_inter[h] * q_state[h][:, dv:]
                   for h in hs}
            inv = {h: 1.0 / jnp.maximum(jnp.abs(den[h]), jnp.exp(-m_row[h])) for h in hs}
            hn = {h: _normalize(_scale_rows(pv[h] + _scale_rows(q_state[h][:, :dv], w_inter[h]),
                                            inv[h])) for h in hs}
            for h in hs:
                cols = slice(h * dv, (h + 1) * dv)
                y_ref[:, cols] = (o_ref[:, cols] * (hn[h] * ng_ref[:, cols])).astype(y_ref.dtype)

            b_last = {h: b_col[h][chunk - 1:chunk, :] for h in hs}
            g = {h: b_last[h] - b_col[h] + i_col[h] for h in hs}
            m_new = {h: jnp.maximum(b_last[h] + m_prev[h], jnp.max(g[h], axis=0, keepdims=True))
                     for h in hs}
            w_state = {h: jnp.exp(g[h] - m_new[h]) * k_scale for h in hs}
            decay = {h: jnp.exp(b_last[h] + m_prev[h] - m_new[h]) for h in hs}
            kw = {h: _scale_rows(k[h].astype(F32), w_state[h]).astype(BF16) for h in hs}
            upd = {h: _dot_tn(kw[h], jnp.concatenate([vh[h], ones], axis=1)) for h in hs}
            for h in hs:
                state_ref[h] = decay[h][:, :1] * state_ref[h] + upd[h]
                m_ref[h] = jnp.broadcast_to(m_new[h], (SUBLANES, LANES))

        for h0 in range(0, heads, group):
            head_group(range(h0, min(h0 + group, heads)))

    return pl.pallas_call(
        body, out_shape=jax.ShapeDtypeStruct((n, heads * dv), BF16), grid=(batch, nc),
        in_specs=[pl.BlockSpec((chunk, heads * dk), lambda b, c: (b * nc + c, 0)),
                  pl.BlockSpec((chunk, heads * dk), lambda b, c: (b * nc + c, 1)),
                  pl.BlockSpec((chunk, heads * dv), lambda b, c: (b * nc + c, 0)),
                  pl.BlockSpec((chunk, heads * dv), lambda b, c: (b * nc + c, 0)),
                  pl.BlockSpec((chunk, 2 * heads * LANES), lambda b, c: (b * nc + c, 0)),
                  pl.BlockSpec((2 * heads, chunk), lambda b, c: (0, b * nc + c)),
                  pl.BlockSpec((None, 1, heads * dv), lambda b, c: (layer, 0, 0))],
        out_specs=pl.BlockSpec((chunk, heads * dv), lambda b, c: (b * nc + c, 0)),
        scratch_shapes=[pltpu.VMEM((heads, dk, dv + LANES), F32),
                        pltpu.VMEM((heads, SUBLANES, LANES), F32)],
        compiler_params=_params("parallel", "arbitrary"), name=name,
    )(qk, qk, v, o_gate, gcol, grow, norm_g)


def _memory_attention(q, kv, *, batch, seq, mem_len, tq, name):
    n, width = q.shape
    hd = width // X_HEADS
    tq = _tile(seq, tq)
    nq = seq // tq
    scale = hd ** -0.5

    def body(q_ref, kv_ref, y_ref):
        for h in range(X_HEADS):
            cols = slice(h * hd, (h + 1) * hd)
            s = _dot_nt(q_ref[:, cols], kv_ref[:, cols]) * scale
            e = jnp.exp(s - jnp.max(s, axis=-1, keepdims=True))
            p = e * (1.0 / jnp.sum(e, axis=-1, keepdims=True))
            y_ref[:, cols] = _dot(p.astype(BF16), kv_ref[:, width + h * hd:width + (h + 1) * hd]
                                  ).astype(y_ref.dtype)

    return pl.pallas_call(
        body, out_shape=jax.ShapeDtypeStruct((n, width), BF16), grid=(batch, nq),
        in_specs=[pl.BlockSpec((tq, width), lambda b, i: (b * nq + i, 0)),
                  pl.BlockSpec((mem_len, 2 * width), lambda b, i: (b, 0))],
        out_specs=pl.BlockSpec((tq, width), lambda b, i: (b * nq + i, 0)),
        compiler_params=_params("parallel", "parallel"), name=name,
    )(q, kv)


def _gated_merge(x, y_a, y_b, y_c, w_g, gate_row0, b_g, gate_col0, w_pa, w_pb, w_pc, layer, *,
                 tm, tn, weight_buffers, name):
    n, d = x.shape
    tm = _tile(n, tm)
    nj = d // tn
    assert gate_col0 % tn == 0
    jg0 = gate_col0 // tn
    wmode = pl.Buffered(weight_buffers)

    def body(x_ref, ya_ref, yb_ref, yc_ref, wg0, wg1, wg2, bg0, bg1, bg2, wa, wb, wc, o_ref,
             wa_b, wb_b, wc_b, wg0_b, wg1_b, wg2_b):
        for w_ref, wb_ref in ((wa, wa_b), (wb, wb_b), (wc, wc_b)):
            _cast_weight_once(w_ref, wb_ref, 1)
        for w_ref, wb_ref in ((wg0, wg0_b), (wg1, wg1_b), (wg2, wg2_b)):
            _cast_weight_once(w_ref, wb_ref, 1, transposed=True)
        xv = x_ref[...]
        acc = _sigmoid(_dot(xv, wg0_b[...]) + bg0[...]) * _dot(ya_ref[...], wa_b[...])
        acc = acc + _sigmoid(_dot(xv, wg1_b[...]) + bg1[...]) * _dot(yb_ref[...], wb_b[...])
        acc = acc + _sigmoid(_dot(xv, wg2_b[...]) + bg2[...]) * _dot(yc_ref[...], wc_b[...])
        o_ref[...] = acc.astype(o_ref.dtype)

    def act_spec(arr):
        return pl.BlockSpec((tm, arr.shape[1]), lambda j, i: (i, 0))

    def gate_w_spec(k):
        return _weight_rows_spec(
            tn, d, lambda j, i: (layer, pl.multiple_of(gate_row0 + k * d + j * tn, SUBLANES), 0),
            wmode)

    def gate_b_spec(k):
        return pl.BlockSpec((None, 1, tn), lambda j, i: (layer, 0, jg0 + k * nj + j))

    def proj_spec(w):
        return pl.BlockSpec((None, w.shape[1], tn), lambda j, i: (layer, 0, j),
                            pipeline_mode=wmode)

    return pl.pallas_call(
        body, out_shape=jax.ShapeDtypeStruct((n, d), BF16), grid=(nj, n // tm),
        in_specs=[act_spec(x), act_spec(y_a), act_spec(y_b), act_spec(y_c),
                  gate_w_spec(0), gate_w_spec(1), gate_w_spec(2),
                  gate_b_spec(0), gate_b_spec(1), gate_b_spec(2),
                  proj_spec(w_pa), proj_spec(w_pb), proj_spec(w_pc)],
        out_specs=pl.BlockSpec((tm, tn), lambda j, i: (i, j)),
        scratch_shapes=([pltpu.VMEM((w.shape[1], tn), BF16) for w in (w_pa, w_pb, w_pc)]
                        + [pltpu.VMEM((d, tn), BF16)] * N_BRANCH),
        compiler_params=_params("parallel", "arbitrary"), name=name,
    )(x, y_a, y_b, y_c, w_g, w_g, w_g, b_g, b_g, b_g, w_pa, w_pb, w_pc)


def _proj_residual_ln(a, w, res, ln_g, ln_b, layer, alpha, *, tm, row_sub, name):
    n, k = a.shape
    d = w.shape[2]
    tm = _tile(n, tm)
    row_sub = min(row_sub, tm)
    assert tm % row_sub == 0

    def body(a_ref, w_ref, r_ref, g_ref, b_ref, of_ref, ob_ref):
        for rb in range(tm // row_sub):
            rows = slice(rb * row_sub, (rb + 1) * row_sub)
            pre = alpha * r_ref[rows, :] + _dot(a_ref[rows, :], w_ref[...])
            y = _normalize(pre) * g_ref[...] + b_ref[...]
            of_ref[rows, :] = y
            ob_ref[rows, :] = y.astype(BF16)

    return pl.pallas_call(
        body,
        out_shape=[jax.ShapeDtypeStruct((n, d), F32), jax.ShapeDtypeStruct((n, d), BF16)],
        grid=(n // tm,),
        in_specs=[pl.BlockSpec((tm, k), lambda i: (i, 0)),
                  pl.BlockSpec((None, k, d), lambda i: (layer, 0, 0), pipeline_mode=pl.Buffered(1)),
                  pl.BlockSpec((tm, d), lambda i: (i, 0)),
                  pl.BlockSpec((None, 1, d), lambda i: (layer, 0, 0)),
                  pl.BlockSpec((None, 1, d), lambda i: (layer, 0, 0))],
        out_specs=[pl.BlockSpec((tm, d), lambda i: (i, 0)), pl.BlockSpec((tm, d), lambda i: (i, 0))],
        compiler_params=_params("parallel"), name=name,
    )(a, w, res, ln_g, ln_b)


def _ffn_up(x, w_gu, layer, d_ff, *, tm, tn, sub, name):
    n, d = x.shape
    tm = _tile(n, tm)
    nj = d_ff // tn
    assert d_ff % tn == 0 and tn % sub == 0

    def body(x_ref, wg_ref, wu_ref, o_ref, wg_b, wu_b):
        _cast_weight_once(wg_ref, wg_b, 1)
        _cast_weight_once(wu_ref, wu_b, 1)
        xv = x_ref[...]
        for sb in range(tn // sub):
            cols = slice(sb * sub, (sb + 1) * sub)
            o_ref[:, cols] = (_silu(_dot(xv, wg_b[:, cols])) * _dot(xv, wu_b[:, cols])
                              ).astype(o_ref.dtype)

    return pl.pallas_call(
        body, out_shape=jax.ShapeDtypeStruct((n, d_ff), BF16), grid=(nj, n // tm),
        in_specs=[pl.BlockSpec((tm, d), lambda j, i: (i, 0)),
                  pl.BlockSpec((None, d, tn), lambda j, i: (layer, 0, j)),
                  pl.BlockSpec((None, d, tn), lambda j, i: (layer, 0, nj + j))],
        out_specs=pl.BlockSpec((tm, tn), lambda j, i: (i, j)),
        scratch_shapes=[pltpu.VMEM((d, tn), BF16), pltpu.VMEM((d, tn), BF16)],
        compiler_params=_params("parallel", "arbitrary"), name=name,
    )(x, w_gu, w_gu)


def kernel(x, mem, mem_ln_g, mem_ln_b, w_in, b_in, g_ln_g, g_ln_b, g_ws, g_bs, m_conv_w, m_conv_b,
           m_norm_g, x_w_kv, w_pa, w_pb, w_pc, w_out, ln1_g, ln1_b, w_gu, w_down, ln2_g, ln2_b):
    batch, seq, d = x.shape
    depth = w_in.shape[0]
    mem_len = mem.shape[1]
    n = batch * seq
    g_width = g_ws.shape[1] * (d // 16)
    qk_width = m_conv_w.shape[2]
    v_width = m_norm_g.shape[1]
    x_width = x_w_kv.shape[2] // 2
    d_ff = w_down.shape[1]
    alpha = (2 * depth) ** 0.25
    heads = M_HEADS

    off_u, off_v = 0, g_width
    off_qk = 2 * g_width
    off_vm = off_qk + qk_width
    off_o = off_vm + v_width
    off_i = off_o + v_width
    off_qx = off_i + 2 * heads
    off_g = off_qx + x_width
    assert w_in.shape[2] == off_g + N_BRANCH * d

    b_in3 = b_in[:, None, :]
    w_in_t = jnp.swapaxes(w_in, 1, 2)
    b_tail = b_in[:, None, off_qx:]
    tail_qx, tail_g = 0, x_width
    w_out_b, w_down_b = w_out.astype(BF16), w_down.astype(BF16)
    g_bs_t = jnp.swapaxes(g_bs, 1, 2)
    row3 = lambda a: a[:, None, :]
    g_ln_g3, g_ln_b3 = row3(g_ln_g), row3(g_ln_b)
    conv_b3, norm_g3 = row3(m_conv_b), row3(m_norm_g)
    ln1_g3, ln1_b3, ln2_g3, ln2_b3 = row3(ln1_g), row3(ln1_b), row3(ln2_g), row3(ln2_b)

    mem_n = _layernorm_rows(mem.reshape(batch * mem_len, d), mem_ln_g, mem_ln_b,
                            tm=256, out_dtype=BF16, name="mem_ln")
    xf = x.reshape(n, d)
    xb = xf.astype(BF16)

    tm_big = _tile(seq, 1024)
    chunk = min(M_SCAN_CHUNK, seq)
    identity = functools.partial(_epi_act, lambda a: a)
    for l in range(depth):
        cfg = LAYER_TILING[l % len(LAYER_TILING)]
        lin = functools.partial(_linear, xb, w_in_t, b_in3, layer=l, tm=tm_big, tn=1024,
                                w_transposed=True)
        u_act, = lin(col0=off_u, ncols=g_width, epilogue=functools.partial(_epi_act, _gelu_tanh),
                     outs=[(g_width, F32)], name=f"l{l}_u")
        v_ln, = lin(col0=off_v, ncols=g_width, epilogue=_epi_gelu_ln, outs=[(g_width, BF16)],
                    extras=[(g_ln_g3, pl.BlockSpec((None, 1, g_width), lambda j, i: (l, 0, 0))),
                            (g_ln_b3, pl.BlockSpec((None, 1, g_width), lambda j, i: (l, 0, 0)))],
                    name=f"l{l}_v")
        qk, = lin(col0=off_qk, ncols=qk_width, epilogue=_epi_conv_silu,
                  prologue=functools.partial(_conv_carry_reset, seq // tm_big),
                  outs=[(qk_width, BF16)],
                  extras=[(m_conv_w, pl.BlockSpec((None, M_CONV, 1024), lambda j, i: (l, 0, j))),
                          (conv_b3, pl.BlockSpec((None, 1, 1024), lambda j, i: (l, 0, j)))],
                  scratch=[pltpu.VMEM((SUBLANES, 1024), F32)], name=f"l{l}_qk")
        v_m, = lin(col0=off_vm, ncols=v_width, epilogue=identity, outs=[(v_width, BF16)],
                   name=f"l{l}_vm")
        o_gate, = lin(col0=off_o, ncols=v_width, epilogue=functools.partial(_epi_act, _sigmoid),
                      outs=[(v_width, F32)], name=f"l{l}_o")
        q_x, = _linear(xb, w_in_t, b_tail, l, tail_qx, x_width, tm=tm_big, tn=1024,
                       w_transposed=True, w_row0=off_qx, epilogue=identity,
                       outs=[(x_width, BF16)], name=f"l{l}_qx")
        gcol, grow = _mlstm_gates(xb, w_in_t, b_in3, l, off_i, tm=tm_big, chunk=chunk,
                                  name=f"l{l}_if")

        y_a = _spatial_gating(u_act, v_ln, g_ws, g_bs_t, l, tm=512, name=f"l{l}_sgu")
        y_b = _mlstm_scan(qk, v_m, o_gate, gcol, grow, norm_g3, l, batch=batch, seq=seq,
                          chunk=chunk, group=M_HEADS, name=f"l{l}_mlstm")
        kv, = _linear(mem_n, x_w_kv, None, l, 0, 2 * x_width, tm=batch * mem_len, tn=1024,
                      epilogue=identity, outs=[(2 * x_width, BF16)], name=f"l{l}_kv")
        y_c = _memory_attention(q_x, kv, batch=batch, seq=seq, mem_len=mem_len, tq=512,
                                name=f"l{l}_xattn")

        merged = _gated_merge(xb, y_a, y_b, y_c, w_in_t, off_g, b_tail, tail_g, w_pa, w_pb, w_pc,
                              l, tm=cfg["merge_tm"], tn=cfg["merge_tn"],
                              weight_buffers=cfg["merge_wbuf"], name=f"l{l}_merge")
        xf, xb = _proj_residual_ln(merged, w_out_b, xf, ln1_g3, ln1_b3, l, alpha, tm=512,
                                   row_sub=128, name=f"l{l}_out_ln1")
        hidden = _ffn_up(xb, w_gu, l, d_ff, tm=tm_big, tn=512, sub=256, name=f"l{l}_ffn_up")
        xf, xb = _proj_residual_ln(hidden, w_down_b, xf, ln2_g3, ln2_b3, l, alpha, tm=256,
                                   row_sub=128, name=f"l{l}_ffn_down_ln2")
    return xf.reshape(batch, seq, d)
```

```python
import functools
import math

import jax
import jax.numpy as jnp
from jax import lax
from jax.experimental import pallas as pl
from jax.experimental.pallas import tpu as pltpu

F32 = jnp.float32
BF16 = jnp.bfloat16

LN_EPS = 1e-5
LANES = 128
SUBLANES = 8
VMEM_LIMIT_BYTES = 48 * 1024 * 1024

G_CHUNK = 128
G_GROUPS = 8
M_HEADS = 4
M_CONV = 4
X_HEADS = 4
N_BRANCH = 3
M_SCAN_CHUNK = 256


def _params(*sem):
    return pltpu.CompilerParams(dimension_semantics=sem, vmem_limit_bytes=VMEM_LIMIT_BYTES)


def _tile(n, pref):
    t = min(n, pref)
    assert n % t == 0, (n, pref)
    return t


def _sigmoid(x):
    return 1.0 / (1.0 + jnp.exp(-x))


def _silu(x):
    return x * _sigmoid(x)


def _gelu_tanh(x):
    c = math.sqrt(2.0 / math.pi)
    return x * (0.5 * (1.0 + jnp.tanh(c * (x + 0.044715 * (x * x * x)))))


def _log_sigmoid(x):
    return jnp.minimum(x, 0.0) - jnp.log1p(jnp.exp(-jnp.abs(x)))


def _normalize(x):
    mu = jnp.mean(x, axis=-1, keepdims=True)
    xc = x - mu
    var = jnp.mean(xc * xc, axis=-1, keepdims=True)
    return xc * lax.rsqrt(var + LN_EPS)


def _dot(a, b):
    return jnp.dot(a, b, preferred_element_type=F32)


def _dot_nt(a, b):
    return lax.dot_general(a, b, (((1,), (1,)), ((), ())), preferred_element_type=F32)


def _dot_tn(a, b):
    return lax.dot_general(a, b, (((0,), (0,)), ((), ())), preferred_element_type=F32)


def _cast_weight_once(w_ref, wb_ref, row_axis, transposed=False):
    @pl.when(pl.program_id(row_axis) == 0)
    def _():
        if transposed:
            w2d = w_ref.at[0] if len(w_ref.shape) == 3 else w_ref
            step = min(w2d.shape[0], 256)
            for r0 in range(0, w2d.shape[0], step):
                wb_ref[:, r0:r0 + step] = w2d[r0:r0 + step, :].T.astype(BF16)
        else:
            wb_ref[...] = w_ref[...].astype(BF16)


def _weight_rows_spec(rows, k, index_map, mode):
    return pl.BlockSpec((pl.Element(1), pl.Element(rows), pl.Element(k)), index_map,
                        pipeline_mode=mode)


def _linear(x, w, bias, layer, col0, ncols, *, tm, tn, epilogue, outs, extras=(), scratch=(),
            prologue=None, w_transposed=False, w_row0=None, name):
    n, k = x.shape
    tm = _tile(n, tm)
    assert ncols % tn == 0 and col0 % tn == 0
    j0 = col0 // tn
    grid = (ncols // tn, n // tm)
    has_bias = bias is not None
    cast_w = w.dtype != BF16
    assert cast_w or not w_transposed
    if w_transposed:
        row0 = col0 if w_row0 is None else w_row0
        assert row0 % SUBLANES == 0
        w_spec = _weight_rows_spec(
            tn, k, lambda j, i: (layer, pl.multiple_of(row0 + j * tn, SUBLANES), 0),
            pl.Buffered(2))
    else:
        w_spec = pl.BlockSpec((None, k, tn), lambda j, i: (layer, 0, j0 + j))
    in_specs = [pl.BlockSpec((tm, k), lambda j, i: (i, 0)), w_spec]
    args = [x, w]
    if has_bias:
        in_specs.append(pl.BlockSpec((None, 1, tn), lambda j, i: (layer, 0, j0 + j)))
        args.append(bias)
    for arr, spec in extras:
        in_specs.append(spec)
        args.append(arr)
    out_shape = [jax.ShapeDtypeStruct((n, tot), dt) for tot, dt in outs]
    out_specs = [pl.BlockSpec((tm, tn), lambda j, i: (i, j)) for _ in outs]
    n_extra, n_out = len(extras), len(outs)
    scratch = list(scratch) + ([pltpu.VMEM((k, tn), BF16)] if cast_w else [])

    def body(*refs):
        x_ref, w_ref = refs[0], refs[1]
        pos = 2
        b_ref = None
        if has_bias:
            b_ref = refs[pos]
            pos += 1
        extra_refs = refs[pos:pos + n_extra]
        out_refs = refs[pos + n_extra:pos + n_extra + n_out]
        scratch_refs = refs[pos + n_extra + n_out:]
        wb_ref = w_ref
        if cast_w:
            wb_ref, scratch_refs = scratch_refs[-1], scratch_refs[:-1]
            _cast_weight_once(w_ref, wb_ref, 1, w_transposed)
        if prologue is not None:
            prologue(scratch_refs)
        acc = _dot(x_ref[...], wb_ref[...])
        if has_bias:
            acc = acc + b_ref[...]
        epilogue(acc, extra_refs, out_refs, scratch_refs)

    res = pl.pallas_call(
        body, out_shape=out_shape, grid=grid, in_specs=in_specs, out_specs=out_specs,
        scratch_shapes=scratch, compiler_params=_params("parallel", "arbitrary"), name=name,
    )(*args)
    return res


def _epi_act(act, acc, extra_refs, out_refs, scratch_refs):
    out_refs[0][...] = act(acc).astype(out_refs[0].dtype)


def _epi_gelu_ln(acc, extra_refs, out_refs, scratch_refs):
    g_ref, b_ref = extra_refs
    y = _normalize(_gelu_tanh(acc)) * g_ref[...] + b_ref[...]
    out_refs[0][...] = y.astype(out_refs[0].dtype)


def _conv_carry_reset(tiles_per_seq, scratch_refs):
    carry_ref, = scratch_refs

    @pl.when(pl.program_id(1) % tiles_per_seq == 0)
    def _():
        carry_ref[...] = jnp.zeros_like(carry_ref)


def _epi_conv_silu(acc, extra_refs, out_refs, scratch_refs):
    cw_ref, cb_ref = extra_refs
    carry_ref, = scratch_refs
    tm = acc.shape[0]
    carry = carry_ref[...]
    cw = cw_ref[...]
    y = acc * cw[M_CONV - 1:M_CONV, :] + cb_ref[...]
    top_rows = lax.broadcasted_iota(jnp.int32, (SUBLANES, acc.shape[1]), 0)
    for s in range(1, M_CONV):
        shifted = pltpu.roll(acc, s, axis=0)
        top = jnp.where(top_rows < s, pltpu.roll(carry, s, axis=0), shifted[:SUBLANES])
        shifted = jnp.concatenate([top, shifted[SUBLANES:]], axis=0)
        y = y + shifted * cw[M_CONV - 1 - s:M_CONV - s, :]
    carry_ref[...] = acc[tm - SUBLANES:, :]
    out_refs[0][...] = _silu(y).astype(out_refs[0].dtype)


def _layernorm_rows(x, g, b, *, tm, out_dtype, name):
    n, d = x.shape
    tm = _tile(n, tm)

    def body(x_ref, g_ref, b_ref, o_ref):
        o_ref[...] = (_normalize(x_ref[...]) * g_ref[...] + b_ref[...]).astype(o_ref.dtype)

    return pl.pallas_call(
        body, out_shape=jax.ShapeDtypeStruct((n, d), out_dtype), grid=(n // tm,),
        in_specs=[pl.BlockSpec((tm, d), lambda i: (i, 0)),
                  pl.BlockSpec((1, d), lambda i: (0, 0)),
                  pl.BlockSpec((1, d), lambda i: (0, 0))],
        out_specs=pl.BlockSpec((tm, d), lambda i: (i, 0)),
        compiler_params=_params("parallel"), name=name,
    )(x, g.reshape(1, d), b.reshape(1, d))


def _split3_bf16(x):
    h1 = x.astype(BF16)
    r1 = x - h1.astype(F32)
    h2 = r1.astype(BF16)
    r2 = r1 - h2.astype(F32)
    return h1, h2, r2.astype(BF16)


def _mlstm_gates(x, w, bias, layer, col0, *, tm, chunk, name):
    n, k = x.shape
    tm = _tile(n, tm)
    assert tm % chunk == 0 and col0 % LANES == 0
    heads = M_HEADS
    rep_w = 2 * heads * LANES

    def body(x_ref, w_ref, b_ref, gcol_ref, grow_ref, wb_ref):
        _cast_weight_once(w_ref, wb_ref, 0, transposed=True)
        z = _dot(x_ref[...], wb_ref[...]) + b_ref[...]
        lf = _log_sigmoid(z)
        r = lax.broadcasted_iota(jnp.int32, (chunk, chunk), 0)
        c = lax.broadcasted_iota(jnp.int32, (chunk, chunk), 1)
        tril = jnp.where(r >= c, 1.0, 0.0).astype(BF16)
        lane = lax.broadcasted_iota(jnp.int32, (chunk, LANES), 1)
        is_f_lane = (lane >= heads) & (lane < 2 * heads)
        for ci in range(tm // chunk):
            sl = slice(ci * chunk, (ci + 1) * chunk)
            a1, a2, a3 = _split3_bf16(lf[sl])
            cs = _dot(tril, a1) + _dot(tril, a2) + _dot(tril, a3)
            gates = jnp.where(is_f_lane, cs, z[sl])
            for c in range(2 * heads):
                gcol_ref[sl, c * LANES:(c + 1) * LANES] = jnp.broadcast_to(
                    gates[:, c:c + 1], (chunk, LANES))
            grow_ref[:, sl] = gates.T[:2 * heads, :]

    return pl.pallas_call(
        body,
        out_shape=[jax.ShapeDtypeStruct((n, rep_w), F32), jax.ShapeDtypeStruct((2 * heads, n), F32)],
        grid=(n // tm,),
        in_specs=[pl.BlockSpec((tm, k), lambda i: (i, 0)),
                  pl.BlockSpec((None, LANES, k), lambda i: (layer, col0 // LANES, 0)),
                  pl.BlockSpec((None, 1, LANES), lambda i: (layer, 0, col0 // LANES))],
        out_specs=[pl.BlockSpec((tm, rep_w), lambda i: (i, 0)),
                   pl.BlockSpec((2 * heads, tm), lambda i: (0, i))],
        scratch_shapes=[pltpu.VMEM((k, LANES), BF16)],
        compiler_params=_params("arbitrary"), name=name,
    )(x, w, bias)


def _spatial_gating(u, v, w_s, b_s_t, layer, *, tm, name):
    n, width = u.shape
    tm = _tile(n, tm)
    gd = width // G_GROUPS
    assert tm % G_CHUNK == 0 and gd % LANES == 0

    def body(u_ref, v_ref, w_ref, b_ref, y_ref):
        r = lax.broadcasted_iota(jnp.int32, (G_CHUNK, G_CHUNK), 0)
        c = lax.broadcasted_iota(jnp.int32, (G_CHUNK, G_CHUNK), 1)
        causal = r >= c
        b_t = b_ref[...]
        for g in range(G_GROUPS):
            w = jnp.where(causal, w_ref[g], 0.0).astype(BF16)
            bias = b_t[:, g:g + 1]
            cols = slice(g * gd, (g + 1) * gd)
            for ci in range(tm // G_CHUNK):
                rows = slice(ci * G_CHUNK, (ci + 1) * G_CHUNK)
                mixed = _dot(w, v_ref[rows, cols]) + bias
                y_ref[rows, cols] = (u_ref[rows, cols] * mixed).astype(y_ref.dtype)

    return pl.pallas_call(
        body, out_shape=jax.ShapeDtypeStruct((n, width), BF16), grid=(n // tm,),
        in_specs=[pl.BlockSpec((tm, width), lambda i: (i, 0)),
                  pl.BlockSpec((tm, width), lambda i: (i, 0)),
                  pl.BlockSpec((None, G_GROUPS, G_CHUNK, G_CHUNK), lambda i: (layer, 0, 0, 0)),
                  pl.BlockSpec((None, G_CHUNK, G_GROUPS), lambda i: (layer, 0, 0))],
        out_specs=pl.BlockSpec((tm, width), lambda i: (i, 0)),
        compiler_params=_params("parallel"), name=name,
    )(u, v, w_s, b_s_t)


def _lane_blocks(a):
    return [a[:, j * LANES:(j + 1) * LANES] for j in range(a.shape[1] // LANES)]


def _scale_rows(a, r):
    return jnp.concatenate([blk * r for blk in _lane_blocks(a)], axis=1)


def _mlstm_scan(qk, v, o_gate, gcol, grow, norm_g, layer, *, batch, seq, chunk, group, name):
    n = batch * seq
    heads = M_HEADS
    dk = qk.shape[1] // (2 * heads)
    dv = v.shape[1] // heads
    nc = seq // chunk
    assert seq % chunk == 0
    k_scale = dk ** -0.5

    def body(q_ref, k_ref, v_ref, o_ref, gcol_ref, grow_ref, ng_ref, y_ref, state_ref, m_ref):
        @pl.when(pl.program_id(1) == 0)
        def _():
            state_ref[...] = jnp.zeros_like(state_ref)
            m_ref[...] = jnp.zeros_like(m_ref)

        grow_v = grow_ref[...]
        r = lax.broadcasted_iota(jnp.int32, (chunk, LANES), 0)
        c = lax.broadcasted_iota(jnp.int32, (chunk, LANES), 1)
        causal = [r >= c + j * LANES for j in range(chunk // LANES)]
        ones = jnp.ones((chunk, LANES), BF16)

        def head_group(hs):
            i_col = {h: gcol_ref[:, h * LANES:(h + 1) * LANES] for h in hs}
            b_col = {h: gcol_ref[:, (heads + h) * LANES:(heads + h + 1) * LANES] for h in hs}
            src_row = {h: _lane_blocks(grow_v[h:h + 1, :] - grow_v[heads + h:heads + h + 1, :])
                       for h in hs}
            m_prev = {h: m_ref[h][0:1, :] for h in hs}
            q = {h: q_ref[:, h * dk:(h + 1) * dk] for h in hs}
            k = {h: k_ref[:, h * dk:(h + 1) * dk] for h in hs}
            vh = {h: v_ref[:, h * dv:(h + 1) * dv] for h in hs}

            scores = {h: _lane_blocks(_dot_nt(q[h], k[h])) for h in hs}
            q_state = {h: _dot(q[h], state_ref[h].astype(BF16)) for h in hs}
            d = {h: [jnp.where(causal[j], b_col[h] + src_row[h][j], -jnp.inf)
                     for j in range(len(causal))] for h in hs}
            m_inter = {h: b_col[h] + m_prev[h] for h in hs}
            m_row = {h: jnp.maximum(
                jnp.max(functools.reduce(jnp.maximum, d[h]), axis=-1, keepdims=True), m_inter[h])
                for h in hs}
            p = {h: jnp.concatenate(
                [jnp.exp(d[h][j] - m_row[h]) * (scores[h][j] * k_scale) for j in range(len(causal))],
                axis=1) for h in hs}
            w_inter = {h: jnp.exp(m_inter[h] - m_row[h]) for h in hs}
            pv = {h: _dot(p[h].astype(BF16), vh[h]) for h in hs}
            den = {h: jnp.sum(p[h], axis=-1, keepdims=True) + w_inter[h] * q_state[h][:, dv:]
                   for h in hs}
            inv = {h: 1.0 / jnp.maximum(jnp.abs(den[h]), jnp.exp(-m_row[h])) for h in hs}
            hn = {h: _normalize(_scale_rows(pv[h] + _scale_rows(q_state[h][:, :dv], w_inter[h]),
                                            inv[h])) for h in hs}
            for h in hs:
                cols = slice(h * dv, (h + 1) * dv)
                y_ref[:, cols] = (o_ref[:, cols] * (hn[h] * ng_ref[:, cols])).astype(y_ref.dtype)

            b_last = {h: b_col[h][chunk - 1:chunk, :] for h in hs}
            g = {h: b_last[h] - b_col[h] + i_col[h] for h in hs}
            m_new = {h: jnp.maximum(b_last[h] + m_prev[h], jnp.max(g[h], axis=0, keepdims=True))
                     for h in hs}
            w_state = {h: jnp.exp(g[h] - m_new[h]) * k_scale for h in hs}
            decay = {h: jnp.exp(b_last[h] + m_prev[h] - m_new[h]) for h in hs}
            kw = {h: _scale_rows(k[h].astype(F32), w_state[h]).astype(BF16) for h in hs}
            upd = {h: _dot_tn(kw[h], jnp.concatenate([vh[h], ones], axis=1)) for h in hs}
            for h in hs:
                state_ref[h] = decay[h][:, :1] * state_ref[h] + upd[h]
                m_ref[h] = jnp.broadcast_to(m_new[h], (SUBLANES, LANES))

        for h0 in range(0, heads, group):
            head_group(range(h0, min(h0 + group, heads)))

    return pl.pallas_call(
        body, out_shape=jax.ShapeDtypeStruct((n, heads * dv), BF16), grid=(batch, nc),
        in_specs=[pl.BlockSpec((chunk, heads * dk), lambda b, c: (b * nc + c, 0)),
                  pl.BlockSpec((chunk, heads * dk), lambda b, c: (b * nc + c, 1)),
                  pl.BlockSpec((chunk, heads * dv), lambda b, c: (b * nc + c, 0)),
                  pl.BlockSpec((chunk, heads * dv), lambda b, c: (b * nc + c, 0)),
                  pl.BlockSpec((chunk, 2 * heads * LANES), lambda b, c: (b * nc + c, 0)),
                  pl.BlockSpec((2 * heads, chunk), lambda b, c: (0, b * nc + c)),
                  pl.BlockSpec((None, 1, heads * dv), lambda b, c: (layer, 0, 0))],
        out_specs=pl.BlockSpec((chunk, heads * dv), lambda b, c: (b * nc + c, 0)),
        scratch_shapes=[pltpu.VMEM((heads, dk, dv + LANES), F32),
                        pltpu.VMEM((heads, SUBLANES, LANES), F32)],
        compiler_params=_params("parallel", "arbitrary"), name=name,
    )(qk, qk, v, o_gate, gcol, grow, norm_g)


def _memory_attention(q, kv, *, batch, seq, mem_len, tq, name):
    n, width = q.shape
    hd = width // X_HEADS
    tq = _tile(seq, tq)
    nq = seq // tq
    scale = hd ** -0.5

    def body(q_ref, kv_ref, y_ref):
        for h in range(X_HEADS):
            cols = slice(h * hd, (h + 1) * hd)
            s = _dot_nt(q_ref[:, cols], kv_ref[:, cols]) * scale
            e = jnp.exp(s - jnp.max(s, axis=-1, keepdims=True))
            p = e * (1.0 / jnp.sum(e, axis=-1, keepdims=True))
            y_ref[:, cols] = _dot(p.astype(BF16), kv_ref[:, width + h * hd:width + (h + 1) * hd]
                                  ).astype(y_ref.dtype)

    return pl.pallas_call(
        body, out_shape=jax.ShapeDtypeStruct((n, width), BF16), grid=(batch, nq),
        in_specs=[pl.BlockSpec((tq, width), lambda b, i: (b * nq + i, 0)),
                  pl.BlockSpec((mem_len, 2 * width), lambda b, i: (b, 0))],
        out_specs=pl.BlockSpec((tq, width), lambda b, i: (b * nq + i, 0)),
        compiler_params=_params("parallel", "parallel"), name=name,
    )(q, kv)


def _gated_merge(x, y_a, y_b, y_c, w_g, gate_row0, b_g, gate_col0, w_pa, w_pb, w_pc, layer, *,
                 tm, tn, name):
    n, d = x.shape
    tm = _tile(n, tm)
    nj = d // tn
    assert gate_col0 % tn == 0
    jg0 = gate_col0 // tn
    wmode = pl.Buffered(1)

    def body(x_ref, ya_ref, yb_ref, yc_ref, wg0, wg1, wg2, bg0, bg1, bg2, wa, wb, wc, o_ref,
             wa_b, wb_b, wc_b, wg0_b, wg1_b, wg2_b):
        for w_ref, wb_ref in ((wa, wa_b), (wb, wb_b), (wc, wc_b)):
            _cast_weight_once(w_ref, wb_ref, 1)
        for w_ref, wb_ref in ((wg0, wg0_b), (wg1, wg1_b), (wg2, wg2_b)):
            _cast_weight_once(w_ref, wb_ref, 1, transposed=True)
        xv = x_ref[...]
        acc = _sigmoid(_dot(xv, wg0_b[...]) + bg0[...]) * _dot(ya_ref[...], wa_b[...])
        acc = acc + _sigmoid(_dot(xv, wg1_b[...]) + bg1[...]) * _dot(yb_ref[...], wb_b[...])
        acc = acc + _sigmoid(_dot(xv, wg2_b[...]) + bg2[...]) * _dot(yc_ref[...], wc_b[...])
        o_ref[...] = acc.astype(o_ref.dtype)

    def act_spec(arr):
        return pl.BlockSpec((tm, arr.shape[1]), lambda j, i: (i, 0))

    def gate_w_spec(k):
        return _weight_rows_spec(
            tn, d, lambda j, i: (layer, pl.multiple_of(gate_row0 + k * d + j * tn, SUBLANES), 0),
            wmode)

    def gate_b_spec(k):
        return pl.BlockSpec((None, 1, tn), lambda j, i: (layer, 0, jg0 + k * nj + j))

    def proj_spec(w):
        return pl.BlockSpec((None, w.shape[1], tn), lambda j, i: (layer, 0, j),
                            pipeline_mode=wmode)

    return pl.pallas_call(
        body, out_shape=jax.ShapeDtypeStruct((n, d), BF16), grid=(nj, n // tm),
        in_specs=[act_spec(x), act_spec(y_a), act_spec(y_b), act_spec(y_c),
                  gate_w_spec(0), gate_w_spec(1), gate_w_spec(2),
                  gate_b_spec(0), gate_b_spec(1), gate_b_spec(2),
                  proj_spec(w_pa), proj_spec(w_pb), proj_spec(w_pc)],
        out_specs=pl.BlockSpec((tm, tn), lambda j, i: (i, j)),
        scratch_shapes=([pltpu.VMEM((w.shape[1], tn), BF16) for w in (w_pa, w_pb, w_pc)]
                        + [pltpu.VMEM((d, tn), BF16)] * N_BRANCH),
        compiler_params=_params("parallel", "arbitrary"), name=name,
    )(x, y_a, y_b, y_c, w_g, w_g, w_g, b_g, b_g, b_g, w_pa, w_pb, w_pc)


def _proj_residual_ln(a, w, res, ln_g, ln_b, layer, alpha, *, tm, row_sub, name):
    n, k = a.shape
    d = w.shape[2]
    tm = _tile(n, tm)
    row_sub = min(row_sub, tm)
    assert tm % row_sub == 0

    def body(a_ref, w_ref, r_ref, g_ref, b_ref, of_ref, ob_ref):
        for rb in range(tm // row_sub):
            rows = slice(rb * row_sub, (rb + 1) * row_sub)
            pre = alpha * r_ref[rows, :] + _dot(a_ref[rows, :], w_ref[...])
            y = _normalize(pre) * g_ref[...] + b_ref[...]
            of_ref[rows, :] = y
            ob_ref[rows, :] = y.astype(BF16)

    return pl.pallas_call(
        body,
        out_shape=[jax.ShapeDtypeStruct((n, d), F32), jax.ShapeDtypeStruct((n, d), BF16)],
        grid=(n // tm,),
        in_specs=[pl.BlockSpec((tm, k), lambda i: (i, 0)),
                  pl.BlockSpec((None, k, d), lambda i: (layer, 0, 0), pipeline_mode=pl.Buffered(1)),
                  pl.BlockSpec((tm, d), lambda i: (i, 0)),
                  pl.BlockSpec((None, 1, d), lambda i: (layer, 0, 0)),
                  pl.BlockSpec((None, 1, d), lambda i: (layer, 0, 0))],
        out_specs=[pl.BlockSpec((tm, d), lambda i: (i, 0)), pl.BlockSpec((tm, d), lambda i: (i, 0))],
        compiler_params=_params("parallel"), name=name,
    )(a, w, res, ln_g, ln_b)


def _ffn_up(x, w_gu, layer, d_ff, *, tm, tn, name):
    n, d = x.shape
    tm = _tile(n, tm)
    nj = d_ff // tn
    assert d_ff % tn == 0

    def body(x_ref, wg_ref, wu_ref, o_ref, wg_b, wu_b):
        _cast_weight_once(wg_ref, wg_b, 1)
        _cast_weight_once(wu_ref, wu_b, 1)
        xv = x_ref[...]
        o_ref[...] = (_silu(_dot(xv, wg_b[...])) * _dot(xv, wu_b[...])).astype(o_ref.dtype)

    return pl.pallas_call(
        body, out_shape=jax.ShapeDtypeStruct((n, d_ff), BF16), grid=(nj, n // tm),
        in_specs=[pl.BlockSpec((tm, d), lambda j, i: (i, 0)),
                  pl.BlockSpec((None, d, tn), lambda j, i: (layer, 0, j)),
                  pl.BlockSpec((None, d, tn), lambda j, i: (layer, 0, nj + j))],
        out_specs=pl.BlockSpec((tm, tn), lambda j, i: (i, j)),
        scratch_shapes=[pltpu.VMEM((d, tn), BF16), pltpu.VMEM((d, tn), BF16)],
        compiler_params=_params("parallel", "arbitrary"), name=name,
    )(x, w_gu, w_gu)


def kernel(x, mem, mem_ln_g, mem_ln_b, w_in, b_in, g_ln_g, g_ln_b, g_ws, g_bs, m_conv_w, m_conv_b,
           m_norm_g, x_w_kv, w_pa, w_pb, w_pc, w_out, ln1_g, ln1_b, w_gu, w_down, ln2_g, ln2_b):
    batch, seq, d = x.shape
    depth = w_in.shape[0]
    mem_len = mem.shape[1]
    n = batch * seq
    g_width = g_ws.shape[1] * (d // 16)
    qk_width = m_conv_w.shape[2]
    v_width = m_norm_g.shape[1]
    x_width = x_w_kv.shape[2] // 2
    d_ff = w_down.shape[1]
    alpha = (2 * depth) ** 0.25
    heads = M_HEADS

    off_u, off_v = 0, g_width
    off_qk = 2 * g_width
    off_vm = off_qk + qk_width
    off_o = off_vm + v_width
    off_i = off_o + v_width
    off_qx = off_i + 2 * heads
    off_g = off_qx + x_width
    assert w_in.shape[2] == off_g + N_BRANCH * d

    b_in3 = b_in[:, None, :]
    w_in_t = jnp.swapaxes(w_in, 1, 2)
    b_tail = b_in[:, None, off_qx:]
    tail_qx, tail_g = 0, x_width
    w_out_b, w_down_b = w_out.astype(BF16), w_down.astype(BF16)
    g_bs_t = jnp.swapaxes(g_bs, 1, 2)
    row3 = lambda a: a[:, None, :]
    g_ln_g3, g_ln_b3 = row3(g_ln_g), row3(g_ln_b)
    conv_b3, norm_g3 = row3(m_conv_b), row3(m_norm_g)
    ln1_g3, ln1_b3, ln2_g3, ln2_b3 = row3(ln1_g), row3(ln1_b), row3(ln2_g), row3(ln2_b)

    mem_n = _layernorm_rows(mem.reshape(batch * mem_len, d), mem_ln_g, mem_ln_b,
                            tm=256, out_dtype=BF16, name="mem_ln")
    xf = x.reshape(n, d)
    xb = xf.astype(BF16)

    tm_big = _tile(seq, 1024)
    chunk = min(M_SCAN_CHUNK, seq)
    identity = functools.partial(_epi_act, lambda a: a)
    for l in range(depth):
        lin = functools.partial(_linear, xb, w_in_t, b_in3, layer=l, tm=tm_big, tn=1024,
                                w_transposed=True)
        u_act, = lin(col0=off_u, ncols=g_width, epilogue=functools.partial(_epi_act, _gelu_tanh),
                     outs=[(g_width, F32)], name=f"l{l}_u")
        v_ln, = lin(col0=off_v, ncols=g_width, epilogue=_epi_gelu_ln, outs=[(g_width, BF16)],
                    extras=[(g_ln_g3, pl.BlockSpec((None, 1, g_width), lambda j, i: (l, 0, 0))),
                            (g_ln_b3, pl.BlockSpec((None, 1, g_width), lambda j, i: (l, 0, 0)))],
                    name=f"l{l}_v")
        qk, = lin(col0=off_qk, ncols=qk_width, epilogue=_epi_conv_silu,
                  prologue=functools.partial(_conv_carry_reset, seq // tm_big),
                  outs=[(qk_width, BF16)],
                  extras=[(m_conv_w, pl.BlockSpec((None, M_CONV, 1024), lambda j, i: (l, 0, j))),
                          (conv_b3, pl.BlockSpec((None, 1, 1024), lambda j, i: (l, 0, j)))],
                  scratch=[pltpu.VMEM((SUBLANES, 1024), F32)], name=f"l{l}_qk")
        v_m, = lin(col0=off_vm, ncols=v_width, epilogue=identity, outs=[(v_width, BF16)],
                   name=f"l{l}_vm")
        o_gate, = lin(col0=off_o, ncols=v_width, epilogue=functools.partial(_epi_act, _sigmoid),
                      outs=[(v_width, F32)], name=f"l{l}_o")
        q_x, = _linear(xb, w_in_t, b_tail, l, tail_qx, x_width, tm=tm_big, tn=1024,
                       w_transposed=True, w_row0=off_qx, epilogue=identity,
                       outs=[(x_width, BF16)], name=f"l{l}_qx")
        gcol, grow = _mlstm_gates(xb, w_in_t, b_in3, l, off_i, tm=tm_big, chunk=chunk,
                                  name=f"l{l}_if")

        y_a = _spatial_gating(u_act, v_ln, g_ws, g_bs_t, l, tm=512, name=f"l{l}_sgu")
        y_b = _mlstm_scan(qk, v_m, o_gate, gcol, grow, norm_g3, l, batch=batch, seq=seq,
                          chunk=chunk, group=M_HEADS, name=f"l{l}_mlstm")
        kv, = _linear(mem_n, x_w_kv, None, l, 0, 2 * x_width, tm=batch * mem_len, tn=1024,
                      epilogue=identity, outs=[(2 * x_width, BF16)], name=f"l{l}_kv")
        y_c = _memory_attention(q_x, kv, batch=batch, seq=seq, mem_len=mem_len, tq=512,
                                name=f"l{l}_xattn")

        merged = _gated_merge(xb, y_a, y_b, y_c, w_in_t, off_g, b_tail, tail_g, w_pa, w_pb, w_pc,
                              l, tm=512, tn=512, name=f"l{l}_merge")
        xf, xb = _proj_residual_ln(merged, w_out_b, xf, ln1_g3, ln1_b3, l, alpha, tm=512,
                                   row_sub=128, name=f"l{l}_out_ln1")
        hidden = _ffn_up(xb, w_gu, l, d_ff, tm=tm_big, tn=512, name=f"l{l}_ffn_up")
        xf, xb = _proj_residual_ln(hidden, w_down_b, xf, ln2_g3, ln2_b3, l, alpha, tm=256,
                                   row_sub=128, name=f"l{l}_ffn_down_ln2")
    return xf.reshape(batch, seq, d)
```

```python
import functools
import math

import jax
import jax.numpy as jnp
from jax import lax
from jax.experimental import pallas as pl
from jax.experimental.pallas import tpu as pltpu

F32 = jnp.float32
BF16 = jnp.bfloat16

LN_EPS = 1e-5
LANES = 128
SUBLANES = 8
VMEM_LIMIT_BYTES = 48 * 1024 * 1024

G_CHUNK = 128
G_GROUPS = 8
M_HEADS = 4
M_CONV = 4
X_HEADS = 4
N_BRANCH = 3
M_SCAN_CHUNK = 256


def _params(*sem):
    return pltpu.CompilerParams(dimension_semantics=sem, vmem_limit_bytes=VMEM_LIMIT_BYTES)


def _tile(n, pref):
    t = min(n, pref)
    assert n % t == 0, (n, pref)
    return t


def _sigmoid(x):
    return 1.0 / (1.0 + jnp.exp(-x))


def _silu(x):
    return x * _sigmoid(x)


def _gelu_tanh(x):
    c = math.sqrt(2.0 / math.pi)
    return x * (0.5 * (1.0 + jnp.tanh(c * (x + 0.044715 * (x * x * x)))))


def _log_sigmoid(x):
    return jnp.minimum(x, 0.0) - jnp.log1p(jnp.exp(-jnp.abs(x)))


def _normalize(x):
    mu = jnp.mean(x, axis=-1, keepdims=True)
    xc = x - mu
    var = jnp.mean(xc * xc, axis=-1, keepdims=True)
    return xc * lax.rsqrt(var + LN_EPS)


def _dot(a, b):
    return jnp.dot(a, b, preferred_element_type=F32)


def _dot_nt(a, b):
    return lax.dot_general(a, b, (((1,), (1,)), ((), ())), preferred_element_type=F32)


def _dot_tn(a, b):
    return lax.dot_general(a, b, (((0,), (0,)), ((), ())), preferred_element_type=F32)


def _cast_weight_once(w_ref, wb_ref, first, transposed=False):
    @pl.when(first)
    def _():
        if transposed:
            w2d = w_ref.at[0] if len(w_ref.shape) == 3 else w_ref
            step = min(w2d.shape[0], 256)
            for r0 in range(0, w2d.shape[0], step):
                wb_ref[:, r0:r0 + step] = w2d[r0:r0 + step, :].T.astype(BF16)
        else:
            wb_ref[...] = w_ref[...].astype(BF16)


def _weight_rows_spec(rows, k, index_map, mode):
    return pl.BlockSpec((pl.Element(1), pl.Element(rows), pl.Element(k)), index_map,
                        pipeline_mode=mode)


def _linear(x, w, bias, layer, col0, ncols, *, tm, tn, epilogue, outs, extras=(), scratch=(),
            prologue=None, w_transposed=False, name):
    n, k = x.shape
    tm = _tile(n, tm)
    assert ncols % tn == 0 and col0 % tn == 0
    j0 = col0 // tn
    grid = (ncols // tn, n // tm)
    has_bias = bias is not None
    cast_w = w.dtype != BF16
    assert cast_w or not w_transposed
    if w_transposed:
        w_spec = _weight_rows_spec(
            tn, k, lambda j, i: (layer, pl.multiple_of(col0 + j * tn, SUBLANES), 0),
            pl.Buffered(2))
    else:
        w_spec = pl.BlockSpec((None, k, tn), lambda j, i: (layer, 0, j0 + j))
    in_specs = [pl.BlockSpec((tm, k), lambda j, i: (i, 0)), w_spec]
    args = [x, w]
    if has_bias:
        in_specs.append(pl.BlockSpec((None, 1, tn), lambda j, i: (layer, 0, j0 + j)))
        args.append(bias)
    for arr, spec in extras:
        in_specs.append(spec)
        args.append(arr)
    out_shape = [jax.ShapeDtypeStruct((n, tot), dt) for tot, dt in outs]
    out_specs = [pl.BlockSpec((tm, tn), lambda j, i: (i, j)) for _ in outs]
    n_extra, n_out = len(extras), len(outs)
    scratch = list(scratch) + ([pltpu.VMEM((k, tn), BF16)] if cast_w else [])

    def body(*refs):
        x_ref, w_ref = refs[0], refs[1]
        pos = 2
        b_ref = None
        if has_bias:
            b_ref = refs[pos]
            pos += 1
        extra_refs = refs[pos:pos + n_extra]
        out_refs = refs[pos + n_extra:pos + n_extra + n_out]
        scratch_refs = refs[pos + n_extra + n_out:]
        wb_ref = w_ref
        if cast_w:
            wb_ref, scratch_refs = scratch_refs[-1], scratch_refs[:-1]
            _cast_weight_once(w_ref, wb_ref, pl.program_id(1) == 0, w_transposed)
        if prologue is not None:
            prologue(scratch_refs)
        acc = _dot(x_ref[...], wb_ref[...])
        if has_bias:
            acc = acc + b_ref[...]
        epilogue(acc, extra_refs, out_refs, scratch_refs)

    res = pl.pallas_call(
        body, out_shape=out_shape, grid=grid, in_specs=in_specs, out_specs=out_specs,
        scratch_shapes=scratch, compiler_params=_params("parallel", "arbitrary"), name=name,
    )(*args)
    return res


def _epi_act(act, acc, extra_refs, out_refs, scratch_refs):
    out_refs[0][...] = act(acc).astype(out_refs[0].dtype)


def _epi_gelu_ln(acc, extra_refs, out_refs, scratch_refs):
    g_ref, b_ref = extra_refs
    y = _normalize(_gelu_tanh(acc)) * g_ref[...] + b_ref[...]
    out_refs[0][...] = y.astype(out_refs[0].dtype)


def _conv_carry_reset(tiles_per_seq, scratch_refs):
    carry_ref, = scratch_refs

    @pl.when(pl.program_id(1) % tiles_per_seq == 0)
    def _():
        carry_ref[...] = jnp.zeros_like(carry_ref)


def _epi_conv_silu(acc, extra_refs, out_refs, scratch_refs):
    cw_ref, cb_ref = extra_refs
    carry_ref, = scratch_refs
    tm = acc.shape[0]
    carry = carry_ref[...]
    cw = cw_ref[...]
    y = acc * cw[M_CONV - 1:M_CONV, :] + cb_ref[...]
    top_rows = lax.broadcasted_iota(jnp.int32, (SUBLANES, acc.shape[1]), 0)
    for s in range(1, M_CONV):
        shifted = pltpu.roll(acc, s, axis=0)
        top = jnp.where(top_rows < s, pltpu.roll(carry, s, axis=0), shifted[:SUBLANES])
        shifted = jnp.concatenate([top, shifted[SUBLANES:]], axis=0)
        y = y + shifted * cw[M_CONV - 1 - s:M_CONV - s, :]
    carry_ref[...] = acc[tm - SUBLANES:, :]
    out_refs[0][...] = _silu(y).astype(out_refs[0].dtype)


def _layernorm_rows(x, g, b, *, tm, out_dtype, name):
    n, d = x.shape
    tm = _tile(n, tm)

    def body(x_ref, g_ref, b_ref, o_ref):
        o_ref[...] = (_normalize(x_ref[...]) * g_ref[...] + b_ref[...]).astype(o_ref.dtype)

    return pl.pallas_call(
        body, out_shape=jax.ShapeDtypeStruct((n, d), out_dtype), grid=(n // tm,),
        in_specs=[pl.BlockSpec((tm, d), lambda i: (i, 0)),
                  pl.BlockSpec((1, d), lambda i: (0, 0)),
                  pl.BlockSpec((1, d), lambda i: (0, 0))],
        out_specs=pl.BlockSpec((tm, d), lambda i: (i, 0)),
        compiler_params=_params("parallel"), name=name,
    )(x, g.reshape(1, d), b.reshape(1, d))


def _split3_bf16(x):
    h1 = x.astype(BF16)
    r1 = x - h1.astype(F32)
    h2 = r1.astype(BF16)
    r2 = r1 - h2.astype(F32)
    return h1, h2, r2.astype(BF16)


def _write_mlstm_gates(z, gcol_ref, grow_ref, chunk):
    heads = M_HEADS
    lf = _log_sigmoid(z)
    r = lax.broadcasted_iota(jnp.int32, (chunk, chunk), 0)
    c = lax.broadcasted_iota(jnp.int32, (chunk, chunk), 1)
    tril = jnp.where(r >= c, 1.0, 0.0).astype(BF16)
    lane = lax.broadcasted_iota(jnp.int32, (chunk, LANES), 1)
    is_f_lane = (lane >= heads) & (lane < 2 * heads)
    for ci in range(z.shape[0] // chunk):
        sl = slice(ci * chunk, (ci + 1) * chunk)
        a1, a2, a3 = _split3_bf16(lf[sl])
        cs = _dot(tril, a1) + _dot(tril, a2) + _dot(tril, a3)
        gates = jnp.where(is_f_lane, cs, z[sl])
        for g in range(2 * heads):
            gcol_ref[sl, g * LANES:(g + 1) * LANES] = jnp.broadcast_to(
                gates[:, g:g + 1], (chunk, LANES))
        grow_ref[:, sl] = gates.T[:2 * heads, :]


def _xattn_and_gates(x, w_t, b_if, if_row0, b_qx, qx_col0, qx_row0, kv, layer, *, batch, seq,
                     mem_len, tm, chunk, name):
    n, k = x.shape
    width = kv.shape[1] // 2
    hd = width // X_HEADS
    heads = M_HEADS
    tm = _tile(seq, tm)
    nq = seq // tm
    scale = hd ** -0.5
    rep_w = 2 * heads * LANES
    assert tm % chunk == 0 and if_row0 % LANES == 0 and qx_row0 % SUBLANES == 0
    assert qx_col0 % width == 0

    def body(x_ref, wq_ref, wg_ref, bq_ref, bg_ref, kv_ref, y_ref, gcol_ref, grow_ref,
             wq_b, wg_b):
        first = (pl.program_id(0) == 0) & (pl.program_id(1) == 0)
        _cast_weight_once(wq_ref, wq_b, first, transposed=True)
        _cast_weight_once(wg_ref, wg_b, first, transposed=True)
        xv = x_ref[...]
        q = (_dot(xv, wq_b[...]) + bq_ref[...]).astype(BF16)
        for h in range(X_HEADS):
            cols = slice(h * hd, (h + 1) * hd)
            s = _dot_nt(q[:, cols], kv_ref[:, cols]) * scale
            e = jnp.exp(s - jnp.max(s, axis=-1, keepdims=True))
            p = e * (1.0 / jnp.sum(e, axis=-1, keepdims=True))
            y_ref[:, cols] = _dot(p.astype(BF16), kv_ref[:, width + h * hd:width + (h + 1) * hd]
                                  ).astype(y_ref.dtype)
        _write_mlstm_gates(_dot(xv, wg_b[...]) + bg_ref[...], gcol_ref, grow_ref, chunk)

    one = pl.Buffered(1)
    return pl.pallas_call(
        body,
        out_shape=[jax.ShapeDtypeStruct((n, width), BF16),
                   jax.ShapeDtypeStruct((n, rep_w), F32),
                   jax.ShapeDtypeStruct((2 * heads, n), F32)],
        grid=(batch, nq),
        in_specs=[pl.BlockSpec((tm, k), lambda b, i: (b * nq + i, 0)),
                  _weight_rows_spec(width, k, lambda b, i: (layer, qx_row0, 0), one),
                  pl.BlockSpec((None, LANES, k), lambda b, i: (layer, if_row0 // LANES, 0),
                               pipeline_mode=one),
                  pl.BlockSpec((None, 1, width), lambda b, i: (layer, 0, qx_col0 // width)),
                  pl.BlockSpec((None, 1, LANES), lambda b, i: (layer, 0, if_row0 // LANES)),
                  pl.BlockSpec((mem_len, 2 * width), lambda b, i: (b, 0))],
        out_specs=[pl.BlockSpec((tm, width), lambda b, i: (b * nq + i, 0)),
                   pl.BlockSpec((tm, rep_w), lambda b, i: (b * nq + i, 0)),
                   pl.BlockSpec((2 * heads, tm), lambda b, i: (0, b * nq + i))],
        scratch_shapes=[pltpu.VMEM((k, width), BF16), pltpu.VMEM((k, LANES), BF16)],
        compiler_params=_params("arbitrary", "arbitrary"), name=name,
    )(x, w_t, w_t, b_qx, b_if, kv)


def _spatial_gating(u, v, w_s, b_s_t, layer, *, tm, name):
    n, width = u.shape
    tm = _tile(n, tm)
    gd = width // G_GROUPS
    assert tm % G_CHUNK == 0 and gd % LANES == 0

    def body(u_ref, v_ref, w_ref, b_ref, y_ref):
        r = lax.broadcasted_iota(jnp.int32, (G_CHUNK, G_CHUNK), 0)
        c = lax.broadcasted_iota(jnp.int32, (G_CHUNK, G_CHUNK), 1)
        causal = r >= c
        b_t = b_ref[...]
        for g in range(G_GROUPS):
            w = jnp.where(causal, w_ref[g], 0.0).astype(BF16)
            bias = b_t[:, g:g + 1]
            cols = slice(g * gd, (g + 1) * gd)
            for ci in range(tm // G_CHUNK):
                rows = slice(ci * G_CHUNK, (ci + 1) * G_CHUNK)
                mixed = _dot(w, v_ref[rows, cols]) + bias
                y_ref[rows, cols] = (u_ref[rows, cols] * mixed).astype(y_ref.dtype)

    return pl.pallas_call(
        body, out_shape=jax.ShapeDtypeStruct((n, width), BF16), grid=(n // tm,),
        in_specs=[pl.BlockSpec((tm, width), lambda i: (i, 0)),
                  pl.BlockSpec((tm, width), lambda i: (i, 0)),
                  pl.BlockSpec((None, G_GROUPS, G_CHUNK, G_CHUNK), lambda i: (layer, 0, 0, 0)),
                  pl.BlockSpec((None, G_CHUNK, G_GROUPS), lambda i: (layer, 0, 0))],
        out_specs=pl.BlockSpec((tm, width), lambda i: (i, 0)),
        compiler_params=_params("parallel"), name=name,
    )(u, v, w_s, b_s_t)


def _lane_blocks(a):
    return [a[:, j * LANES:(j + 1) * LANES] for j in range(a.shape[1] // LANES)]


def _scale_rows(a, r):
    return jnp.concatenate([blk * r for blk in _lane_blocks(a)], axis=1)


def _mlstm_scan(qk, v, o_gate, gcol, grow, norm_g, layer, *, batch, seq, chunk, group, name):
    n = batch * seq
    heads = M_HEADS
    dk = qk.shape[1] // (2 * heads)
    dv = v.shape[1] // heads
    nc = seq // chunk
    assert seq % chunk == 0
    k_scale = dk ** -0.5

    def body(q_ref, k_ref, v_ref, o_ref, gcol_ref, grow_ref, ng_ref, y_ref, state_ref, m_ref):
        @pl.when(pl.program_id(1) == 0)
        def _():
            state_ref[...] = jnp.zeros_like(state_ref)
            m_ref[...] = jnp.zeros_like(m_ref)

        grow_v = grow_ref[...]
        r = lax.broadcasted_iota(jnp.int32, (chunk, LANES), 0)
        c = lax.broadcasted_iota(jnp.int32, (chunk, LANES), 1)
        causal = [r >= c + j * LANES for j in range(chunk // LANES)]
        ones = jnp.ones((chunk, LANES), BF16)

        def head_group(hs):
            i_col = {h: gcol_ref[:, h * LANES:(h + 1) * LANES] for h in hs}
            b_col = {h: gcol_ref[:, (heads + h) * LANES:(heads + h + 1) * LANES] for h in hs}
            src_row = {h: _lane_blocks(grow_v[h:h + 1, :] - grow_v[heads + h:heads + h + 1, :])
                       for h in hs}
            m_prev = {h: m_ref[h][0:1, :] for h in hs}
            q = {h: q_ref[:, h * dk:(h + 1) * dk] for h in hs}
            k = {h: k_ref[:, h * dk:(h + 1) * dk] for h in hs}
            vh = {h: v_ref[:, h * dv:(h + 1) * dv] for h in hs}

            scores = {h: _lane_blocks(_dot_nt(q[h], k[h])) for h in hs}
            q_state = {h: _dot(q[h], state_ref[h].astype(BF16)) for h in hs}
            d = {h: [jnp.where(causal[j], b_col[h] + src_row[h][j], -jnp.inf)
                     for j in range(len(causal))] for h in hs}
            m_inter = {h: b_col[h] + m_prev[h] for h in hs}
            m_row = {h: jnp.maximum(
                jnp.max(functools.reduce(jnp.maximum, d[h]), axis=-1, keepdims=True), m_inter[h])
                for h in hs}
            p = {h: jnp.concatenate(
                [jnp.exp(d[h][j] - m_row[h]) * (scores[h][j] * k_scale) for j in range(len(causal))],
                axis=1) for h in hs}
            w_inter = {h: jnp.exp(m_inter[h] - m_row[h]) for h in hs}
            pv = {h: _dot(p[h].astype(BF16), vh[h]) for h in hs}
            den = {h: jnp.sum(p[h], axis=-1, keepdims=True) + w_inter[h] * q_state[h][:, dv:]
                   for h in hs}
            inv = {h: 1.0 / jnp.maximum(jnp.abs(den[h]), jnp.exp(-m_row[h])) for h in hs}
            hn = {h: _normalize(_scale_rows(pv[h] + _scale_rows(q_state[h][:, :dv], w_inter[h]),
                                            inv[h])) for h in hs}
            for h in hs:
                cols = slice(h * dv, (h + 1) * dv)
                y_ref[:, cols] = (o_ref[:, cols] * (hn[h] * ng_ref[:, cols])).astype(y_ref.dtype)

            b_last = {h: b_col[h][chunk - 1:chunk, :] for h in hs}
            g = {h: b_last[h] - b_col[h] + i_col[h] for h in hs}
            m_new = {h: jnp.maximum(b_last[h] + m_prev[h], jnp.max(g[h], axis=0, keepdims=True))
                     for h in hs}
            w_state = {h: jnp.exp(g[h] - m_new[h]) * k_scale for h in hs}
            decay = {h: jnp.exp(b_last[h] + m_prev[h] - m_new[h]) for h in hs}
            kw = {h: _scale_rows(k[h].astype(F32), w_state[h]).astype(BF16) for h in hs}
            upd = {h: _dot_tn(kw[h], jnp.concatenate([vh[h], ones], axis=1)) for h in hs}
            for h in hs:
                state_ref[h] = decay[h][:, :1] * state_ref[h] + upd[h]
                m_ref[h] = jnp.broadcast_to(m_new[h], (SUBLANES, LANES))

        for h0 in range(0, heads, group):
            head_group(range(h0, min(h0 + group, heads)))

    return pl.pallas_call(
        body, out_shape=jax.ShapeDtypeStruct((n, heads * dv), BF16), grid=(batch, nc),
        in_specs=[pl.BlockSpec((chunk, heads * dk), lambda b, c: (b * nc + c, 0)),
                  pl.BlockSpec((chunk, heads * dk), lambda b, c: (b * nc + c, 1)),
                  pl.BlockSpec((chunk, heads * dv), lambda b, c: (b * nc + c, 0)),
                  pl.BlockSpec((chunk, heads * dv), lambda b, c: (b * nc + c, 0)),
                  pl.BlockSpec((chunk, 2 * heads * LANES), lambda b, c: (b * nc + c, 0)),
                  pl.BlockSpec((2 * heads, chunk), lambda b, c: (0, b * nc + c)),
                  pl.BlockSpec((None, 1, heads * dv), lambda b, c: (layer, 0, 0))],
        out_specs=pl.BlockSpec((chunk, heads * dv), lambda b, c: (b * nc + c, 0)),
        scratch_shapes=[pltpu.VMEM((heads, dk, dv + LANES), F32),
                        pltpu.VMEM((heads, SUBLANES, LANES), F32)],
        compiler_params=_params("parallel", "arbitrary"), name=name,
    )(qk, qk, v, o_gate, gcol, grow, norm_g)


def _gated_merge(x, y_a, y_b, y_c, w_g, gate_row0, b_g, gate_col0, w_pa, w_pb, w_pc, layer, *,
                 tm, tn, name):
    n, d = x.shape
    tm = _tile(n, tm)
    nj = d // tn
    assert gate_col0 % tn == 0
    jg0 = gate_col0 // tn
    wmode = pl.Buffered(1)

    def body(x_ref, ya_ref, yb_ref, yc_ref, wg0, wg1, wg2, bg0, bg1, bg2, wa, wb, wc, o_ref,
             wa_b, wb_b, wc_b, wg0_b, wg1_b, wg2_b):
        for w_ref, wb_ref in ((wa, wa_b), (wb, wb_b), (wc, wc_b)):
            _cast_weight_once(w_ref, wb_ref, pl.program_id(1) == 0)
        for w_ref, wb_ref in ((wg0, wg0_b), (wg1, wg1_b), (wg2, wg2_b)):
            _cast_weight_once(w_ref, wb_ref, pl.program_id(1) == 0, transposed=True)
        xv = x_ref[...]
        acc = _sigmoid(_dot(xv, wg0_b[...]) + bg0[...]) * _dot(ya_ref[...], wa_b[...])
        acc = acc + _sigmoid(_dot(xv, wg1_b[...]) + bg1[...]) * _dot(yb_ref[...], wb_b[...])
        acc = acc + _sigmoid(_dot(xv, wg2_b[...]) + bg2[...]) * _dot(yc_ref[...], wc_b[...])
        o_ref[...] = acc.astype(o_ref.dtype)

    def act_spec(arr):
        return pl.BlockSpec((tm, arr.shape[1]), lambda j, i: (i, 0))

    def gate_w_spec(k):
        return _weight_rows_spec(
            tn, d, lambda j, i: (layer, pl.multiple_of(gate_row0 + k * d + j * tn, SUBLANES), 0),
            wmode)

    def gate_b_spec(k):
        return pl.BlockSpec((None, 1, tn), lambda j, i: (layer, 0, jg0 + k * nj + j))

    def proj_spec(w):
        return pl.BlockSpec((None, w.shape[1], tn), lambda j, i: (layer, 0, j),
                            pipeline_mode=wmode)

    return pl.pallas_call(
        body, out_shape=jax.ShapeDtypeStruct((n, d), BF16), grid=(nj, n // tm),
        in_specs=[act_spec(x), act_spec(y_a), act_spec(y_b), act_spec(y_c),
                  gate_w_spec(0), gate_w_spec(1), gate_w_spec(2),
                  gate_b_spec(0), gate_b_spec(1), gate_b_spec(2),
                  proj_spec(w_pa), proj_spec(w_pb), proj_spec(w_pc)],
        out_specs=pl.BlockSpec((tm, tn), lambda j, i: (i, j)),
        scratch_shapes=([pltpu.VMEM((w.shape[1], tn), BF16) for w in (w_pa, w_pb, w_pc)]
                        + [pltpu.VMEM((d, tn), BF16)] * N_BRANCH),
        compiler_params=_params("parallel", "arbitrary"), name=name,
    )(x, y_a, y_b, y_c, w_g, w_g, w_g, b_g, b_g, b_g, w_pa, w_pb, w_pc)


def _proj_residual_ln(a, w, res, ln_g, ln_b, layer, alpha, *, tm, row_sub, name):
    n, k = a.shape
    d = w.shape[2]
    tm = _tile(n, tm)
    row_sub = min(row_sub, tm)
    assert tm % row_sub == 0

    def body(a_ref, w_ref, r_ref, g_ref, b_ref, of_ref, ob_ref):
        for rb in range(tm // row_sub):
            rows = slice(rb * row_sub, (rb + 1) * row_sub)
            pre = alpha * r_ref[rows, :] + _dot(a_ref[rows, :], w_ref[...])
            y = _normalize(pre) * g_ref[...] + b_ref[...]
            of_ref[rows, :] = y
            ob_ref[rows, :] = y.astype(BF16)

    return pl.pallas_call(
        body,
        out_shape=[jax.ShapeDtypeStruct((n, d), F32), jax.ShapeDtypeStruct((n, d), BF16)],
        grid=(n // tm,),
        in_specs=[pl.BlockSpec((tm, k), lambda i: (i, 0)),
                  pl.BlockSpec((None, k, d), lambda i: (layer, 0, 0), pipeline_mode=pl.Buffered(1)),
                  pl.BlockSpec((tm, d), lambda i: (i, 0)),
                  pl.BlockSpec((None, 1, d), lambda i: (layer, 0, 0)),
                  pl.BlockSpec((None, 1, d), lambda i: (layer, 0, 0))],
        out_specs=[pl.BlockSpec((tm, d), lambda i: (i, 0)), pl.BlockSpec((tm, d), lambda i: (i, 0))],
        compiler_params=_params("parallel"), name=name,
    )(a, w, res, ln_g, ln_b)


def _ffn_up(x, w_gu, layer, d_ff, *, tm, tn, name):
    n, d = x.shape
    tm = _tile(n, tm)
    nj = d_ff // tn
    assert d_ff % tn == 0

    def body(x_ref, wg_ref, wu_ref, o_ref, wg_b, wu_b):
        _cast_weight_once(wg_ref, wg_b, pl.program_id(1) == 0)
        _cast_weight_once(wu_ref, wu_b, pl.program_id(1) == 0)
        xv = x_ref[...]
        o_ref[...] = (_silu(_dot(xv, wg_b[...])) * _dot(xv, wu_b[...])).astype(o_ref.dtype)

    return pl.pallas_call(
        body, out_shape=jax.ShapeDtypeStruct((n, d_ff), BF16), grid=(nj, n // tm),
        in_specs=[pl.BlockSpec((tm, d), lambda j, i: (i, 0)),
                  pl.BlockSpec((None, d, tn), lambda j, i: (layer, 0, j)),
                  pl.BlockSpec((None, d, tn), lambda j, i: (layer, 0, nj + j))],
        out_specs=pl.BlockSpec((tm, tn), lambda j, i: (i, j)),
        scratch_shapes=[pltpu.VMEM((d, tn), BF16), pltpu.VMEM((d, tn), BF16)],
        compiler_params=_params("parallel", "arbitrary"), name=name,
    )(x, w_gu, w_gu)


def kernel(x, mem, mem_ln_g, mem_ln_b, w_in, b_in, g_ln_g, g_ln_b, g_ws, g_bs, m_conv_w, m_conv_b,
           m_norm_g, x_w_kv, w_pa, w_pb, w_pc, w_out, ln1_g, ln1_b, w_gu, w_down, ln2_g, ln2_b):
    batch, seq, d = x.shape
    depth = w_in.shape[0]
    mem_len = mem.shape[1]
    n = batch * seq
    g_width = g_ws.shape[1] * (d // 16)
    qk_width = m_conv_w.shape[2]
    v_width = m_norm_g.shape[1]
    x_width = x_w_kv.shape[2] // 2
    d_ff = w_down.shape[1]
    alpha = (2 * depth) ** 0.25
    heads = M_HEADS

    off_u, off_v = 0, g_width
    off_qk = 2 * g_width
    off_vm = off_qk + qk_width
    off_o = off_vm + v_width
    off_i = off_o + v_width
    off_qx = off_i + 2 * heads
    off_g = off_qx + x_width
    assert w_in.shape[2] == off_g + N_BRANCH * d

    b_in3 = b_in[:, None, :]
    w_in_t = jnp.swapaxes(w_in, 1, 2)
    b_tail = b_in[:, None, off_qx:]
    tail_qx, tail_g = 0, x_width
    w_out_b, w_down_b = w_out.astype(BF16), w_down.astype(BF16)
    g_bs_t = jnp.swapaxes(g_bs, 1, 2)
    row3 = lambda a: a[:, None, :]
    g_ln_g3, g_ln_b3 = row3(g_ln_g), row3(g_ln_b)
    conv_b3, norm_g3 = row3(m_conv_b), row3(m_norm_g)
    ln1_g3, ln1_b3, ln2_g3, ln2_b3 = row3(ln1_g), row3(ln1_b), row3(ln2_g), row3(ln2_b)

    mem_n = _layernorm_rows(mem.reshape(batch * mem_len, d), mem_ln_g, mem_ln_b,
                            tm=256, out_dtype=BF16, name="mem_ln")
    xf = x.reshape(n, d)
    xb = xf.astype(BF16)

    tm_big = _tile(seq, 1024)
    chunk = min(M_SCAN_CHUNK, seq)
    identity = functools.partial(_epi_act, lambda a: a)
    for l in range(depth):
        lin = functools.partial(_linear, xb, w_in_t, b_in3, layer=l, tm=tm_big, tn=1024,
                                w_transposed=True)
        u_act, = lin(col0=off_u, ncols=g_width, epilogue=functools.partial(_epi_act, _gelu_tanh),
                     outs=[(g_width, F32)], name=f"l{l}_u")
        v_ln, = lin(col0=off_v, ncols=g_width, epilogue=_epi_gelu_ln, outs=[(g_width, BF16)],
                    extras=[(g_ln_g3, pl.BlockSpec((None, 1, g_width), lambda j, i: (l, 0, 0))),
                            (g_ln_b3, pl.BlockSpec((None, 1, g_width), lambda j, i: (l, 0, 0)))],
                    name=f"l{l}_v")
        qk, = lin(col0=off_qk, ncols=qk_width, epilogue=_epi_conv_silu,
                  prologue=functools.partial(_conv_carry_reset, seq // tm_big),
                  outs=[(qk_width, BF16)],
                  extras=[(m_conv_w, pl.BlockSpec((None, M_CONV, 1024), lambda j, i: (l, 0, j))),
                          (conv_b3, pl.BlockSpec((None, 1, 1024), lambda j, i: (l, 0, j)))],
                  scratch=[pltpu.VMEM((SUBLANES, 1024), F32)], name=f"l{l}_qk")
        v_m, = lin(col0=off_vm, ncols=v_width, epilogue=identity, outs=[(v_width, BF16)],
                   name=f"l{l}_vm")
        o_gate, = lin(col0=off_o, ncols=v_width, epilogue=functools.partial(_epi_act, _sigmoid),
                      outs=[(v_width, F32)], name=f"l{l}_o")
        kv, = _linear(mem_n, x_w_kv, None, l, 0, 2 * x_width, tm=batch * mem_len, tn=1024,
                      epilogue=identity, outs=[(2 * x_width, BF16)], name=f"l{l}_kv")
        y_c, gcol, grow = _xattn_and_gates(xb, w_in_t, b_in3, off_i, b_tail, tail_qx, off_qx, kv,
                                           l, batch=batch, seq=seq, mem_len=mem_len, tm=tm_big,
                                           chunk=chunk, name=f"l{l}_xattn_if")

        y_a = _spatial_gating(u_act, v_ln, g_ws, g_bs_t, l, tm=512, name=f"l{l}_sgu")
        y_b = _mlstm_scan(qk, v_m, o_gate, gcol, grow, norm_g3, l, batch=batch, seq=seq,
                          chunk=chunk, group=M_HEADS, name=f"l{l}_mlstm")

        merged = _gated_merge(xb, y_a, y_b, y_c, w_in_t, off_g, b_tail, tail_g, w_pa, w_pb, w_pc,
                              l, tm=512, tn=512, name=f"l{l}_merge")
        xf, xb = _proj_residual_ln(merged, w_out_b, xf, ln1_g3, ln1_b3, l, alpha, tm=512,
                                   row_sub=128, name=f"l{l}_out_ln1")
        hidden = _ffn_up(xb, w_gu, l, d_ff, tm=tm_big, tn=512, name=f"l{l}_ffn_up")
        xf, xb = _proj_residual_ln(hidden, w_down_b, xf, ln2_g3, ln2_b3, l, alpha, tm=256,
                                   row_sub=128, name=f"l{l}_ffn_down_ln2")
    return xf.reshape(batch, seq, d)
```
